```python
import jax, jax.numpy as jnp
from jax import lax
import numpy as np

D_MODEL = 2048
BATCH = 4
SEQ = 2048
DEPTH = 4
DEC_BATCH = 128
DEC_SEQ = 4
PAST_LEN = 16384
PAGE_SIZE = 128

D_MIX = 2 * D_MODEL
D_SSD = D_MIX // 2
SSD_HEAD_DIM = 64
SSD_HEADS = D_SSD // SSD_HEAD_DIM
SSD_GROUPS = 4
SSD_STATE = 128
SSD_CHUNK = 128
CONV_W = 4
CONV_DIM = D_SSD + 2 * SSD_GROUPS * SSD_STATE
D_CM = D_MIX - D_SSD
CM_GROUPS = 16
CM_GROUP_DIM = D_CM // CM_GROUPS
CM_CHUNK = 128
D_FF = -(-(8 * D_MODEL) // (3 * 256)) * 256
D_PLE = 256
D_IN = 2 * D_SSD + 2 * SSD_GROUPS * SSD_STATE + SSD_HEADS + 2 * D_CM
EPS = 1e-6

kernel_name = "hybrid_ssd_chunkmlp_decoder_step"


def rmsnorm(x, g):
    xf = x.astype(jnp.float32)
    y = xf * lax.rsqrt(jnp.mean(xf * xf, axis=-1, keepdims=True) + EPS)
    return (y * g.astype(jnp.float32)).astype(x.dtype)


def causal_dwconv(xbc, buf, w, b):
    xp = jnp.concatenate([buf.astype(xbc.dtype), xbc], axis=1)
    y = lax.conv_general_dilated(
        xp, w[:, None, :].astype(xbc.dtype), window_strides=(1,), padding='VALID',
        dimension_numbers=('NWC', 'WIO', 'NWC'), feature_group_count=xbc.shape[-1])
    return y + b.astype(xbc.dtype), xp[:, -(CONV_W - 1):]


def ssd(x, dt, A, Bm, Cm, h0):
    b, L = x.shape[:2]
    Q = min(SSD_CHUNK, L)
    nc = L // Q
    G, Hg, P, N = SSD_GROUPS, SSD_HEADS // SSD_GROUPS, SSD_HEAD_DIM, SSD_STATE
    f32 = jnp.float32
    xr = x.astype(f32).reshape(b, nc, Q, G, Hg, P)
    dtr = dt.astype(f32).reshape(b, nc, Q, G, Hg)
    Br = Bm.astype(f32).reshape(b, nc, Q, G, N)
    Cr = Cm.astype(f32).reshape(b, nc, Q, G, N)
    acum = jnp.cumsum(dtr * A.astype(f32).reshape(G, Hg), axis=2)
    xdt = xr * dtr[..., None]
    mask = jnp.tril(jnp.ones((Q, Q), dtype=bool))[:, :, None, None]
    seg = acum[:, :, :, None] - acum[:, :, None, :]
    decay = jnp.exp(jnp.where(mask, seg, -jnp.inf))
    cb = jnp.einsum('bclgn,bcsgn->bclsg', Cr, Br)
    y_diag = jnp.einsum('bclsgh,bcsghp->bclghp', cb[..., None] * decay, xdt)
    decay_s = jnp.exp(acum[:, :, -1:] - acum)
    states = jnp.einsum('bcsgn,bcsgh,bcsghp->bcghpn', Br, decay_s, xdt)
    chunk_decay = jnp.exp(acum[:, :, -1])

    def step(h, inp):
        st, dc = inp
        return h * dc[..., None, None] + st, h

    hT, prev = lax.scan(step, h0.astype(f32).reshape(b, G, Hg, P, N),
                        (jnp.moveaxis(states, 1, 0), jnp.moveaxis(chunk_decay, 1, 0)))
    prev = jnp.moveaxis(prev, 0, 1)
    y_off = jnp.einsum('bclgn,bcghpn,bclgh->bclghp', Cr, prev, jnp.exp(acum))
    y = (y_diag + y_off).reshape(b, L, SSD_HEADS, P)
    return y, hT.reshape(b, SSD_HEADS, P, N)


def chunk_mlp(u, v, w_s, b_s):
    b, L = u.shape[:2]
    Q = min(CM_CHUNK, L)
    nc = L // Q
    w = w_s[:, :Q, :Q] * jnp.tril(jnp.ones((Q, Q), dtype=w_s.dtype))
    vr = v.reshape(b, nc, Q, CM_GROUPS, CM_GROUP_DIM)
    s = jnp.einsum('gts,bcsgd->bctgd', w, vr) + b_s[:, :Q].T[None, None, :, :, None]
    return u * s.reshape(b, L, D_CM)


def layer(h, p_l, ssm0, conv0, g_mix, w_in, conv_w, conv_b, dt_bias, a_log, d_skip,
          g_ssd, g_cm, w_s, b_s, w_out, g_ffn, w_gate, w_up, w_down, g_pg, w_pg, w_ple):
    b, L, _ = h.shape
    n = rmsnorm(h, g_mix)
    proj = n @ w_in
    i1 = D_SSD
    i2 = i1 + CONV_DIM
    i3 = i2 + SSD_HEADS
    i4 = i3 + D_CM
    z, xbc, dt_raw, u, v = (proj[..., :i1], proj[..., i1:i2], proj[..., i2:i3],
                            proj[..., i3:i4], proj[..., i4:])
    xbc_c, conv_new = causal_dwconv(xbc, conv0, conv_w, conv_b)
    xbc_c = jax.nn.silu(xbc_c)
    gn = SSD_GROUPS * SSD_STATE
    xs = xbc_c[..., :D_SSD].reshape(b, L, SSD_HEADS, SSD_HEAD_DIM)
    Bm = xbc_c[..., D_SSD:D_SSD + gn].reshape(b, L, SSD_GROUPS, SSD_STATE)
    Cm = xbc_c[..., D_SSD + gn:].reshape(b, L, SSD_GROUPS, SSD_STATE)
    dt = jax.nn.softplus(dt_raw.astype(jnp.float32) + dt_bias.astype(jnp.float32))
    A = -jnp.exp(a_log.astype(jnp.float32))
    y, ssm_new = ssd(xs, dt, A, Bm, Cm, ssm0)
    y = y + d_skip.astype(jnp.float32)[:, None] * xs.astype(jnp.float32)
    y = y.reshape(b, L, SSD_GROUPS, D_SSD // SSD_GROUPS) * \
        jax.nn.silu(z.astype(jnp.float32)).reshape(b, L, SSD_GROUPS, D_SSD // SSD_GROUPS)
    y_ssd = rmsnorm(y, g_ssd.reshape(SSD_GROUPS, -1)).reshape(b, L, D_SSD).astype(h.dtype)
    u = jax.nn.gelu(u)
    v = rmsnorm(jax.nn.gelu(v), g_cm)
    y_cm = chunk_mlp(u, v, w_s, b_s)
    h = h + jnp.concatenate([y_ssd, y_cm.astype(h.dtype)], axis=-1) @ w_out
    f = rmsnorm(h, g_ffn)
    h = h + (jax.nn.silu(f @ w_gate) * (f @ w_up)) @ w_down
    gate = jax.nn.sigmoid(rmsnorm(h, g_pg) @ w_pg)
    h = h + (p_l @ w_ple) * gate
    return h, ssm_new, conv_new, v


def setup_inputs(seed: int = 0) -> dict:
    key = jax.random.key(seed)
    ks = iter(jax.random.split(key, 32))
    nrm = lambda shape, s=1.0: s * jax.random.normal(next(ks), shape, jnp.float32)
    gain = lambda shape: 1.0 + 0.02 * jax.random.normal(next(ks), shape, jnp.float32)
    dt0 = jnp.exp(jax.random.uniform(next(ks), (DEPTH, SSD_HEADS), jnp.float32,
                                     np.log(1e-3), np.log(1e-1)))
    return {
        "x_prompt": nrm((BATCH, SEQ, D_MODEL)),
        "x_sample": nrm((DEC_BATCH, DEC_SEQ, D_MODEL)),
        "state_ssm": nrm((DEPTH, DEC_BATCH, SSD_HEADS, SSD_HEAD_DIM, SSD_STATE), 0.1),
        "state_conv": nrm((DEPTH, DEC_BATCH, CONV_W - 1, CONV_DIM)),
        "p_prompt": nrm((DEPTH, BATCH, SEQ, D_PLE)),
        "p_sample": nrm((DEPTH, DEC_BATCH, DEC_SEQ, D_PLE)),
        "g_mix": gain((DEPTH, D_MODEL)),
        "w_in": nrm((DEPTH, D_MODEL, D_IN), D_MODEL ** -0.5),
        "conv_w": nrm((DEPTH, CONV_W, CONV_DIM), CONV_W ** -0.5),
        "conv_b": nrm((DEPTH, CONV_DIM), 0.02),
        "dt_bias": dt0 + jnp.log(-jnp.expm1(-dt0)),
        "a_log": jnp.log(jax.random.uniform(next(ks), (DEPTH, SSD_HEADS), jnp.float32, 1.0, 16.0)),
        "d_skip": gain((DEPTH, SSD_HEADS)),
        "g_ssd": gain((DEPTH, D_SSD)),
        "g_cm": gain((DEPTH, D_CM)),
        "w_s": nrm((DEPTH, CM_GROUPS, CM_CHUNK, CM_CHUNK), 0.05),
        "b_s": 1.0 + nrm((DEPTH, CM_GROUPS, CM_CHUNK), 0.1),
        "w_out": nrm((DEPTH, D_MIX, D_MODEL), D_MIX ** -0.5),
        "g_ffn": gain((DEPTH, D_MODEL)),
        "w_gate": nrm((DEPTH, D_MODEL, D_FF), D_MODEL ** -0.5),
        "w_up": nrm((DEPTH, D_MODEL, D_FF), D_MODEL ** -0.5),
        "w_down": nrm((DEPTH, D_FF, D_MODEL), D_FF ** -0.5),
        "g_pg": gain((DEPTH, D_MODEL)),
        "w_pg": nrm((DEPTH, D_MODEL, D_MODEL), D_MODEL ** -0.5),
        "w_ple": nrm((DEPTH, D_PLE, D_MODEL), D_PLE ** -0.5),
        "g_final": gain((D_MODEL,)),
    }


def reference(x_prompt, x_sample, state_ssm, state_conv, p_prompt, p_sample, g_mix, w_in,
              conv_w, conv_b, dt_bias, a_log, d_skip, g_ssd, g_cm, w_s, b_s, w_out, g_ffn,
              w_gate, w_up, w_down, g_pg, w_pg, w_ple, g_final):
    bp = x_prompt.shape[0]
    ssm_zero = jnp.zeros((bp, SSD_HEADS, SSD_HEAD_DIM, SSD_STATE), x_prompt.dtype)
    conv_zero = jnp.zeros((bp, CONV_W - 1, CONV_DIM), x_prompt.dtype)
    hp, hs = x_prompt, x_sample
    ssm_p, conv_p, ssm_s, conv_s, v_s = [], [], [], [], []
    for i in range(DEPTH):
        lw = (g_mix[i], w_in[i], conv_w[i], conv_b[i], dt_bias[i], a_log[i], d_skip[i],
              g_ssd[i], g_cm[i], w_s[i], b_s[i], w_out[i], g_ffn[i], w_gate[i], w_up[i],
              w_down[i], g_pg[i], w_pg[i], w_ple[i])
        hp, sp, cp, _ = layer(hp, p_prompt[i], ssm_zero, conv_zero, *lw)
        hs, ss, cs, vs = layer(hs, p_sample[i], state_ssm[i], state_conv[i], *lw)
        ssm_p.append(sp); conv_p.append(cp)
        ssm_s.append(ss); conv_s.append(cs); v_s.append(vs)
    y_prompt = rmsnorm(hp, g_final)
    y_sample = rmsnorm(hs, g_final)
    return (y_prompt, y_sample, jnp.stack(ssm_p), jnp.stack(conv_p),
            jnp.stack(ssm_s), jnp.stack(conv_s), jnp.stack(v_s))
```

```python
import functools

import jax
import jax.numpy as jnp
import numpy as np
from jax import lax
from jax.experimental import pallas as pl
from jax.experimental.pallas import tpu as pltpu

F32 = jnp.float32
BF16 = jnp.bfloat16

D_MODEL = 2048
BATCH = 4
SEQ = 2048
DEPTH = 4
DEC_BATCH = 128
DEC_SEQ = 4
D_SSD = 2048
HEAD_DIM = 64
HEADS = 32
GROUPS = 4
HEADS_PER_GROUP = HEADS // GROUPS
GROUP_W = D_SSD // GROUPS
STATE = 128
CONV_W = 4
BC_W = 2 * GROUPS * STATE
D_CM = 2048
CM_GROUPS = 16
D_FF = 5632
D_PLE = 256
EPS = 1e-6

T_PROMPT = BATCH * SEQ
T_SAMPLE = DEC_BATCH * DEC_SEQ
T_ALL = T_PROMPT + T_SAMPLE
TILE = 128
LANES = 128
N_MAIN = 4 * 2048 + BC_W

VMEM_LIMIT = 56 * 1024 * 1024


def _cparams(n_axes):
    return pltpu.CompilerParams(dimension_semantics=("arbitrary",) * n_axes,
                                vmem_limit_bytes=VMEM_LIMIT)


def _rms_cast_kernel(x_ref, g_ref, o_ref):
    x = x_ref[...]
    y = x * lax.rsqrt(jnp.mean(x * x, axis=-1, keepdims=True) + EPS)
    o_ref[...] = (y * g_ref[...]).astype(o_ref.dtype)


def _rms(x, g, out_dtype, tm=544):
    m, d = x.shape
    return pl.pallas_call(
        _rms_cast_kernel,
        grid=(m // tm,),
        in_specs=[pl.BlockSpec((tm, d), lambda i: (i, 0)),
                  pl.BlockSpec((1, d), lambda i: (0, 0))],
        out_specs=pl.BlockSpec((tm, d), lambda i: (i, 0)),
        out_shape=jax.ShapeDtypeStruct((m, d), out_dtype),
        compiler_params=_cparams(1),
        name="rms_" + jnp.dtype(out_dtype).name,
    )(x, g.reshape(1, d))


def _mm_kernel(x_ref, w_ref, o_ref, wb_ref):
    @pl.when(pl.program_id(1) == 0)
    def _():
        wb_ref[...] = w_ref[...].astype(BF16)

    o_ref[...] = jnp.dot(x_ref[...], wb_ref[...], preferred_element_type=F32)


def _mm(x, w, tm, tn, name):
    m, k = x.shape
    n = w.shape[1]
    return pl.pallas_call(
        _mm_kernel,
        grid=(n // tn, m // tm),
        in_specs=[pl.BlockSpec((tm, k), lambda j, i: (i, 0)),
                  pl.BlockSpec((k, tn), lambda j, i: (0, j))],
        out_specs=pl.BlockSpec((tm, tn), lambda j, i: (i, j)),
        out_shape=jax.ShapeDtypeStruct((m, n), F32),
        scratch_shapes=[pltpu.VMEM((k, tn), BF16)],
        compiler_params=_cparams(2),
        name=name,
    )(x, w)


def _mm_res_kernel(x_ref, w_ref, h_ref, o_ref, wb_ref):
    @pl.when(pl.program_id(1) == 0)
    def _():
        wb_ref[...] = w_ref[...].astype(BF16)

    o_ref[...] = h_ref[...] + jnp.dot(x_ref[...], wb_ref[...], preferred_element_type=F32)


def _mm_res(x, w, h, tm, tn, name):
    m, k = x.shape
    n = w.shape[1]
    return pl.pallas_call(
        _mm_res_kernel,
        grid=(n // tn, m // tm),
        in_specs=[pl.BlockSpec((tm, k), lambda j, i: (i, 0)),
                  pl.BlockSpec((k, tn), lambda j, i: (0, j)),
                  pl.BlockSpec((tm, tn), lambda j, i: (i, j))],
        out_specs=pl.BlockSpec((tm, tn), lambda j, i: (i, j)),
        out_shape=jax.ShapeDtypeStruct((m, n), F32),
        scratch_shapes=[pltpu.VMEM((k, tn), BF16)],
        compiler_params=_cparams(2),
        name=name,
    )(x, w, h)


def _ffn_up_kernel(x_ref, wg_ref, wu_ref, o_ref, wgb_ref, wub_ref):
    @pl.when(pl.program_id(1) == 0)
    def _():
        wgb_ref[...] = wg_ref[...].astype(BF16)
        wub_ref[...] = wu_ref[...].astype(BF16)

    x = x_ref[...]
    gate = jnp.dot(x, wgb_ref[...], preferred_element_type=F32)
    up = jnp.dot(x, wub_ref[...], preferred_element_type=F32)
    o_ref[...] = (jax.nn.silu(gate) * up).astype(o_ref.dtype)


def _ffn_up(x, wg, wu, tm, tn):
    m, k = x.shape
    n = wg.shape[1]
    return pl.pallas_call(
        _ffn_up_kernel,
        grid=(n // tn, m // tm),
        in_specs=[pl.BlockSpec((tm, k), lambda j, i: (i, 0)),
                  pl.BlockSpec((k, tn), lambda j, i: (0, j)),
                  pl.BlockSpec((k, tn), lambda j, i: (0, j))],
        out_specs=pl.BlockSpec((tm, tn), lambda j, i: (i, j)),
        out_shape=jax.ShapeDtypeStruct((m, n), BF16),
        scratch_shapes=[pltpu.VMEM((k, tn), BF16), pltpu.VMEM((k, tn), BF16)],
        compiler_params=_cparams(2),
        name="ffn_up",
    )(x, wg, wu)


def _ple_kernel(n_ref, p_ref, h_ref, wpg_ref, wple_ref, o_ref, wpgb_ref, wpleb_ref):
    @pl.when(pl.program_id(1) == 0)
    def _():
        wpgb_ref[...] = wpg_ref[...].astype(BF16)
        wpleb_ref[...] = wple_ref[...].astype(BF16)

    gate = jax.nn.sigmoid(jnp.dot(n_ref[...], wpgb_ref[...], preferred_element_type=F32))
    emb = jnp.dot(p_ref[...].astype(BF16), wpleb_ref[...], preferred_element_type=F32)
    o_ref[...] = h_ref[...] + emb * gate


def _ple(n, p, h, wpg, wple, tm, tn):
    m, k = n.shape
    kp = p.shape[1]
    nn = wpg.shape[1]
    return pl.pallas_call(
        _ple_kernel,
        grid=(nn // tn, m // tm),
        in_specs=[pl.BlockSpec((tm, k), lambda j, i: (i, 0)),
                  pl.BlockSpec((tm, kp), lambda j, i: (i, 0)),
                  pl.BlockSpec((tm, tn), lambda j, i: (i, j)),
                  pl.BlockSpec((k, tn), lambda j, i: (0, j)),
                  pl.BlockSpec((kp, tn), lambda j, i: (0, j))],
        out_specs=pl.BlockSpec((tm, tn), lambda j, i: (i, j)),
        out_shape=jax.ShapeDtypeStruct((m, nn), F32),
        scratch_shapes=[pltpu.VMEM((k, tn), BF16), pltpu.VMEM((kp, tn), BF16)],
        compiler_params=_cparams(2),
        name="ple",
    )(n, p, h, wpg, wple)


def _softplus(x):
    return jnp.maximum(x, 0.0) + jnp.log1p(jnp.exp(-jnp.abs(x)))


def _expand_heads(v, e2_ref):
    lane = lax.broadcasted_iota(jnp.int32, v.shape, 1)
    v = jnp.where(lane < HEADS, v, 0.0)
    hi = v.astype(BF16)
    lo = (v - hi.astype(F32)).astype(BF16)
    return jnp.dot(jnp.concatenate([hi, lo], axis=1), e2_ref[...], preferred_element_type=F32)


def _mixer_kernel(*refs, mode):
    prompt = mode == "prompt"
    prep = mode == "sample_prep"
    it = iter(refs)
    z_ref, x_ref, u_ref, v_ref, bc_ref, dt_ref = (next(it) for _ in range(6))
    if not prompt:
        hx_refs = [next(it) for _ in range(3)]
        hbc_refs = [next(it) for _ in range(3)]
    if mode == "sample_final":
        yofft_ref = next(it)
    (cwx_ref, cwbc_ref, cbx_ref, cbbc_ref, dtb_ref, alog_ref, dsk_ref, gssd_ref, gcm_ref,
     e2_ref, ws_ref, bexp_ref) = (next(it) for _ in range(12))
    if prompt:
        y_ref, ssm_ref = next(it), next(it)
        st_ref, extx_ref, extbc_ref, xs_ref = (next(it) for _ in range(4))
    elif prep:
        xst_ref, bm_ref, cmt_ref, ea_ref = (next(it) for _ in range(4))
        extx_ref, extbc_ref, xs_ref = (next(it) for _ in range(3))
    else:
        y_ref, vn_ref = next(it), next(it)
        extx_ref, extbc_ref, xs_ref = (next(it) for _ in range(3))

    row = lax.broadcasted_iota(jnp.int32, (TILE, LANES), 0)
    col = lax.broadcasted_iota(jnp.int32, (TILE, LANES), 1)
    if prompt:
        pair_mask = col <= row
    else:
        pair_mask = (col <= row) & ((col >> 2) == (row >> 2))

    if prompt:
        @pl.when(pl.program_id(1) == 0)
        def _():
            st_ref[...] = jnp.zeros_like(st_ref)
            extx_ref[0:8, :] = jnp.zeros((8, D_SSD), F32)
            extbc_ref[0:8, :] = jnp.zeros((8, BC_W), F32)

    def conv(src_ref, ext_ref, w_ref, b_ref, hist_refs, width):
        if not prompt:
            ext_ref[0:8, :] = jnp.zeros((8, width), F32)
        ext_ref[8:8 + TILE, :] = src_ref[...]
        acc = b_ref[...] + w_ref[CONV_W - 1:CONV_W, :] * ext_ref[8:8 + TILE, :]
        rw = lax.broadcasted_iota(jnp.int32, (TILE, width), 0)
        for j in range(1, CONV_W):
            shifted = ext_ref[8 - j:8 - j + TILE, :]
            if not prompt:
                shifted = jnp.where((rw & (DEC_SEQ - 1)) >= j, shifted, hist_refs[j - 1][...])
            acc = acc + w_ref[CONV_W - 1 - j:CONV_W - j, :] * shifted
        if prompt:
            ext_ref[0:8, :] = ext_ref[TILE:TILE + 8, :]
        return jax.nn.silu(acc)

    xs_ref[...] = conv(x_ref, extx_ref, cwx_ref, cbx_ref, None if prompt else hx_refs, D_SSD)
    bcs = conv(bc_ref, extbc_ref, cwbc_ref, cbbc_ref, None if prompt else hbc_refs, BC_W)
    bm = [bcs[:, g * STATE:(g + 1) * STATE] for g in range(GROUPS)]
    cm = [bcs[:, (GROUPS + g) * STATE:(GROUPS + g + 1) * STATE] for g in range(GROUPS)]

    dt = _softplus(dt_ref[...] + dtb_ref[...])
    dta = dt * (-jnp.exp(alog_ref[...]))
    acum = dta
    if prompt:
        for s in (1, 2, 4, 8, 16, 32, 64):
            acum = acum + jnp.where(row >= s, pltpu.roll(acum, s, axis=0), 0.0)
        alast = acum[TILE - 1:TILE, :]
    else:
        pos = row & (DEC_SEQ - 1)
        for s in (1, 2):
            acum = acum + jnp.where(pos >= s, pltpu.roll(acum, s, axis=0), 0.0)
        alast = jnp.where(pos == DEC_SEQ - 1, acum, 0.0)
        alast = alast + jnp.where(pos == DEC_SEQ - 2, pltpu.roll(alast, TILE - 1, axis=0), 0.0)
        alast = alast + jnp.where(pos < DEC_SEQ - 2, pltpu.roll(alast, TILE - 2, axis=0), 0.0)
    ea = jnp.exp(acum)
    dsc = dt * jnp.exp(alast - acum)
    dsc_e = _expand_heads(dsc, e2_ref)

    if prep:
        xst_ref[...] = (xs_ref[...] * dsc_e).T.astype(BF16)
        bm_ref[...] = bcs[:, :GROUPS * STATE]
        for g in range(GROUPS):
            cmt_ref[g * STATE:(g + 1) * STATE, :] = cm[g].T
        ea_ref[...] = ea
        return

    dt_e = _expand_heads(dt, e2_ref)
    ea_e = _expand_heads(ea, e2_ref)
    acum_t = acum.T

    cb = [lax.dot_general(cm[g].astype(BF16), bm[g].astype(BF16), (((1,), (1,)), ((), ())),
                          preferred_element_type=F32) for g in range(GROUPS)]
    ydiag = []
    for j in range(HEADS // 2):
        w_pair = []
        for h in (2 * j, 2 * j + 1):
            seg = acum[:, h:h + 1] - acum_t[h:h + 1, :]
            dec = jnp.exp(jnp.where(pair_mask, seg, -jnp.inf))
            w_pair.append((cb[h // HEADS_PER_GROUP] * dec).astype(BF16))
        xdt = (xs_ref[:, j * LANES:(j + 1) * LANES] * dt_e[:, j * LANES:(j + 1) * LANES]).astype(BF16)
        lo_half = col < HEAD_DIM
        rhs = jnp.concatenate([jnp.where(lo_half, xdt, jnp.zeros_like(xdt)),
                               jnp.where(lo_half, jnp.zeros_like(xdt), xdt)], axis=0)
        ydiag.append(jnp.dot(jnp.concatenate(w_pair, axis=1), rhs, preferred_element_type=F32))

    if prompt:
        yoff = [jnp.dot(cm[g].astype(BF16), st_ref[:, g * GROUP_W:(g + 1) * GROUP_W].astype(BF16),
                        preferred_element_type=F32) for g in range(GROUPS)]
        for g in range(GROUPS):
            sl = slice(g * GROUP_W, (g + 1) * GROUP_W)
            xsc = (xs_ref[:, sl] * dsc_e[:, sl]).astype(BF16)
            upd = jnp.dot(bm[g].T.astype(BF16), xsc, preferred_element_type=F32)
            st_ref[:, sl] = st_ref[:, sl] * ea_e[TILE - 1:TILE, sl] + upd

        @pl.when(pl.program_id(1) == pl.num_programs(1) - 1)
        def _():
            ssm_ref[0] = st_ref[...].T
    else:
        yoff_all = yofft_ref[...].T
        yoff = [yoff_all[:, g * GROUP_W:(g + 1) * GROUP_W] for g in range(GROUPS)]

    for g in range(GROUPS):
        sl = slice(g * GROUP_W, (g + 1) * GROUP_W)
        yd = jnp.concatenate(ydiag[g * 4:(g + 1) * 4], axis=1)
        y = yd + yoff[g] * ea_e[:, sl] + dsk_ref[:, sl] * xs_ref[:, sl]
        y = y * jax.nn.silu(z_ref[:, sl])
        y = y * lax.rsqrt(jnp.mean(y * y, axis=-1, keepdims=True) + EPS)
        y_ref[:, sl] = (y * gssd_ref[:, sl]).astype(y_ref.dtype)

    gv = jax.nn.gelu(v_ref[...])
    vn = gv * lax.rsqrt(jnp.mean(gv * gv, axis=-1, keepdims=True) + EPS) * gcm_ref[...]
    if not prompt:
        vn_ref[...] = vn
    for g in range(CM_GROUPS):
        sl = slice(g * LANES, (g + 1) * LANES)
        wm = jnp.where(pair_mask, ws_ref[g], 0.0).astype(BF16)
        s = jnp.dot(wm, vn[:, sl].astype(BF16), preferred_element_type=F32) + bexp_ref[:, sl]
        y_ref[:, D_SSD + g * LANES:D_SSD + (g + 1) * LANES] = (
            jax.nn.gelu(u_ref[:, sl]) * s).astype(y_ref.dtype)


def _state_kernel(dec_ref, st_ref, xst_ref, bm_ref, cmt_ref, new_ref, yt_ref):
    b = pl.program_id(0)
    per_tile = TILE // DEC_SEQ
    bl = b % per_tile

    @pl.when(bl == 0)
    def _():
        yt_ref[...] = jnp.zeros_like(yt_ref)

    row = lax.broadcasted_iota(jnp.int32, (TILE, LANES), 0)
    col = lax.broadcasted_iota(jnp.int32, (TILE, LANES), 1)
    mine_rows = (row >> 2) == bl
    mine_cols = (col >> 2) == bl
    for g in range(GROUPS):
        rs = slice(g * GROUP_W, (g + 1) * GROUP_W)
        sg = st_ref[0, rs, :]
        ctm = jnp.where(mine_cols, cmt_ref[g * STATE:(g + 1) * STATE, :], 0.0).astype(BF16)
        yt_ref[rs, :] += jnp.dot(sg.astype(BF16), ctm, preferred_element_type=F32)
        bmk = jnp.where(mine_rows, bm_ref[:, g * STATE:(g + 1) * STATE], 0.0).astype(BF16)
        upd = jnp.dot(xst_ref[rs, :], bmk, preferred_element_type=F32)
        for hh in range(HEADS_PER_GROUP):
            h = g * HEADS_PER_GROUP + hh
            hs = slice(hh * HEAD_DIM, (hh + 1) * HEAD_DIM)
            new_ref[0, g * GROUP_W + hh * HEAD_DIM:g * GROUP_W + (hh + 1) * HEAD_DIM, :] = (
                sg[hs, :] * dec_ref[b, h] + upd[hs, :])


def _const_spec(shape):
    nd = len(shape)
    return pl.BlockSpec(shape, lambda *idx: (0,) * nd)


def _param_specs():
    return [_const_spec((CONV_W, D_SSD)), _const_spec((CONV_W, BC_W)), _const_spec((1, D_SSD)),
            _const_spec((1, BC_W)), _const_spec((1, LANES)), _const_spec((1, LANES)),
            _const_spec((1, D_SSD)), _const_spec((1, D_SSD)), _const_spec((1, D_CM)),
            _const_spec((2 * LANES, D_SSD)), _const_spec((CM_GROUPS, TILE, TILE)),
            _const_spec((TILE, D_CM))]


def _mix_prompt(proj, dtraw, params):
    tiles = SEQ // TILE

    def rt(b, c):
        return b * tiles + c

    in_specs = [pl.BlockSpec((TILE, 2048), lambda b, c: (rt(b, c), 0)),
                pl.BlockSpec((TILE, 2048), lambda b, c: (rt(b, c), 1)),
                pl.BlockSpec((TILE, 2048), lambda b, c: (rt(b, c), 2)),
                pl.BlockSpec((TILE, 2048), lambda b, c: (rt(b, c), 3)),
                pl.BlockSpec((TILE, BC_W), lambda b, c: (rt(b, c), 8)),
                pl.BlockSpec((TILE, LANES), lambda b, c: (rt(b, c), 0))]
    in_specs += _param_specs()
    return pl.pallas_call(
        functools.partial(_mixer_kernel, mode="prompt"),
        grid=(BATCH, tiles),
        in_specs=in_specs,
        out_specs=[pl.BlockSpec((TILE, 2 * D_SSD), lambda b, c: (rt(b, c), 0)),
                   pl.BlockSpec((1, D_SSD, STATE), lambda b, c: (b, 0, 0))],
        out_shape=[jax.ShapeDtypeStruct((T_PROMPT, 2 * D_SSD), BF16),
                   jax.ShapeDtypeStruct((BATCH, D_SSD, STATE), F32)],
        scratch_shapes=[pltpu.VMEM((STATE, D_SSD), F32),
                        pltpu.VMEM((TILE + 8, D_SSD), F32),
                        pltpu.VMEM((TILE + 8, BC_W), F32),
                        pltpu.VMEM((TILE, D_SSD), F32)],
        compiler_params=_cparams(2),
        name="mix_prompt",
    )(proj, proj, proj, proj, proj, dtraw, *params)


def _sample_in_specs():
    base = T_PROMPT // TILE
    return [pl.BlockSpec((TILE, 2048), lambda t: (base + t, 0)),
            pl.BlockSpec((TILE, 2048), lambda t: (base + t, 1)),
            pl.BlockSpec((TILE, 2048), lambda t: (base + t, 2)),
            pl.BlockSpec((TILE, 2048), lambda t: (base + t, 3)),
            pl.BlockSpec((TILE, BC_W), lambda t: (base + t, 8)),
            pl.BlockSpec((TILE, LANES), lambda t: (base + t, 0))]


def _hist_specs():
    return ([pl.BlockSpec((TILE, D_SSD), lambda t: (t, 0))] * 3
            + [pl.BlockSpec((TILE, BC_W), lambda t: (t, 0))] * 3)


def _sample_scratch():
    return [pltpu.VMEM((TILE + 8, D_SSD), F32), pltpu.VMEM((TILE + 8, BC_W), F32),
            pltpu.VMEM((TILE, D_SSD), F32)]


def _mix_sample_prep(proj, dtraw, hists, params):
    nt = T_SAMPLE // TILE
    return pl.pallas_call(
        functools.partial(_mixer_kernel, mode="sample_prep"),
        grid=(nt,),
        in_specs=_sample_in_specs() + _hist_specs() + _param_specs(),
        out_specs=[pl.BlockSpec((D_SSD, TILE), lambda t: (0, t)),
                   pl.BlockSpec((TILE, GROUPS * STATE), lambda t: (t, 0)),
                   pl.BlockSpec((GROUPS * STATE, TILE), lambda t: (0, t)),
                   pl.BlockSpec((TILE, LANES), lambda t: (t, 0))],
        out_shape=[jax.ShapeDtypeStruct((D_SSD, T_SAMPLE), BF16),
                   jax.ShapeDtypeStruct((T_SAMPLE, GROUPS * STATE), F32),
                   jax.ShapeDtypeStruct((GROUPS * STATE, T_SAMPLE), F32),
                   jax.ShapeDtypeStruct((T_SAMPLE, LANES), F32)],
        scratch_shapes=_sample_scratch(),
        compiler_params=_cparams(1),
        name="mix_sample_prep",
    )(proj, proj, proj, proj, proj, dtraw, *hists, *params)


def _mix_sample_final(proj, dtraw, hists, yofft, params):
    nt = T_SAMPLE // TILE
    return pl.pallas_call(
        functools.partial(_mixer_kernel, mode="sample_final"),
        grid=(nt,),
        in_specs=(_sample_in_specs() + _hist_specs()
                  + [pl.BlockSpec((D_SSD, TILE), lambda t: (0, t))] + _param_specs()),
        out_specs=[pl.BlockSpec((TILE, 2 * D_SSD), lambda t: (t, 0)),
                   pl.BlockSpec((TILE, D_CM), lambda t: (t, 0))],
        out_shape=[jax.ShapeDtypeStruct((T_SAMPLE, 2 * D_SSD), BF16),
                   jax.ShapeDtypeStruct((T_SAMPLE, D_CM), F32)],
        scratch_shapes=_sample_scratch(),
        compiler_params=_cparams(1),
        name="mix_sample_final",
    )(proj, proj, proj, proj, proj, dtraw, *hists, yofft, *params)


def _sample_state(dec, state, xst, bm, cmt):
    per_tile = TILE // DEC_SEQ
    return pl.pallas_call(
        _state_kernel,
        grid=(DEC_BATCH,),
        in_specs=[pl.BlockSpec(memory_space=pltpu.SMEM),
                  pl.BlockSpec((1, D_SSD, STATE), lambda b: (b, 0, 0)),
                  pl.BlockSpec((D_SSD, TILE), lambda b: (0, b // per_tile)),
                  pl.BlockSpec((TILE, GROUPS * STATE), lambda b: (b // per_tile, 0)),
                  pl.BlockSpec((GROUPS * STATE, TILE), lambda b: (0, b // per_tile))],
        out_specs=[pl.BlockSpec((1, D_SSD, STATE), lambda b: (b, 0, 0)),
                   pl.BlockSpec((D_SSD, TILE), lambda b: (0, b // per_tile))],
        out_shape=[jax.ShapeDtypeStruct((DEC_BATCH, D_SSD, STATE), F32),
                   jax.ShapeDtypeStruct((D_SSD, T_SAMPLE), F32)],
        compiler_params=_cparams(1),
        name="sample_state",
    )(dec, state, xst, bm, cmt)


def _expansion_matrix():
    e = np.zeros((2 * LANES, D_SSD), np.float32)
    for h in range(HEADS):
        e[h, h * HEAD_DIM:(h + 1) * HEAD_DIM] = 1.0
        e[LANES + h, h * HEAD_DIM:(h + 1) * HEAD_DIM] = 1.0
    return jnp.asarray(e, BF16)


def _pad_lanes(v):
    return jnp.pad(v, (0, LANES - v.shape[0])).reshape(1, LANES)


def kernel(x_prompt, x_sample, state_ssm, state_conv, p_prompt, p_sample, g_mix, w_in, conv_w, conv_b, dt_bias, a_log, d_skip, g_ssd, g_cm, w_s, b_s, w_out, g_ffn, w_gate, w_up, w_down, g_pg, w_pg, w_ple, g_final):
    h = jnp.concatenate([x_prompt.reshape(T_PROMPT, D_MODEL), x_sample.reshape(T_SAMPLE, D_MODEL)], axis=0)
    e2 = _expansion_matrix()
    i_x, i_dt, i_u = D_SSD, D_SSD + D_SSD + BC_W, D_SSD + D_SSD + BC_W + HEADS
    ssm_p, conv_p, ssm_s, conv_s, v_s = [], [], [], [], []
    for i in range(DEPTH):
        wi = w_in[i]
        w_main = jnp.concatenate([wi[:, :i_x + D_SSD], wi[:, i_u:], wi[:, i_x + D_SSD:i_dt]], axis=1)
        w_dt = jnp.pad(wi[:, i_dt:i_u], ((0, 0), (0, LANES - HEADS)))
        params_common = (conv_w[i][:, :D_SSD], conv_w[i][:, D_SSD:], conv_b[i][:D_SSD].reshape(1, -1),
                         conv_b[i][D_SSD:].reshape(1, -1), _pad_lanes(dt_bias[i]), _pad_lanes(a_log[i]),
                         jnp.repeat(d_skip[i], HEAD_DIM).reshape(1, -1), g_ssd[i].reshape(1, -1),
                         g_cm[i].reshape(1, -1), e2)
        params_p = params_common + (w_s[i], jnp.repeat(b_s[i].T, LANES, axis=1))
        reps = TILE // DEC_SEQ
        params_s = params_common + (jnp.tile(w_s[i][:, :DEC_SEQ, :DEC_SEQ], (1, reps, reps)),
                                    jnp.tile(jnp.repeat(b_s[i][:, :DEC_SEQ].T, LANES, axis=1), (reps, 1)))
        sc = state_conv[i]
        hists_x, hists_bc = [], []
        for j in range(1, CONV_W):
            hj = jnp.pad(sc[:, CONV_W - 1 - j:, :], ((0, 0), (0, DEC_SEQ - j), (0, 0))).reshape(T_SAMPLE, -1)
            hists_x.append(hj[:, :D_SSD])
            hists_bc.append(hj[:, D_SSD:])
        hists = hists_x + hists_bc

        n = _rms(h, g_mix[i], BF16)
        proj = _mm(n, w_main, 1088, 1024, "in_proj")
        dtraw = _mm(n, w_dt, 1088, LANES, "in_proj_dt")

        y_p, ssm_new_p = _mix_prompt(proj, dtraw, params_p)
        xst, bmat, cmt, ea = _mix_sample_prep(proj, dtraw, hists, params_s)
        dec = ea[DEC_SEQ - 1::DEC_SEQ, :HEADS]
        ssm_new_s, yofft = _sample_state(dec, state_ssm[i].reshape(DEC_BATCH, D_SSD, STATE), xst, bmat, cmt)
        y_s, vn_s = _mix_sample_final(proj, dtraw, hists, yofft, params_s)
        y = jnp.concatenate([y_p, y_s], axis=0)

        h = _mm_res(y, w_out[i], h, 544, 512, "out_proj")
        f = _rms(h, g_ffn[i], BF16)
        act = _ffn_up(f, w_gate[i], w_up[i], 1088, 512)
        h = _mm_res(act, w_down[i], h, 544, 512, "ffn_down")
        npg = _rms(h, g_pg[i], BF16)
        p_all = jnp.concatenate([p_prompt[i].reshape(T_PROMPT, D_PLE), p_sample[i].reshape(T_SAMPLE, D_PLE)], axis=0)
        h = _ple(npg, p_all, h, w_pg[i], w_ple[i], 1088, 512)

        xbc = jnp.concatenate([proj[:, i_x:i_x + D_SSD], proj[:, 4 * 2048:]], axis=1)
        ssm_p.append(ssm_new_p.reshape(BATCH, HEADS, HEAD_DIM, STATE))
        conv_p.append(xbc[:T_PROMPT].reshape(BATCH, SEQ, -1)[:, -(CONV_W - 1):])
        ssm_s.append(ssm_new_s.reshape(DEC_BATCH, HEADS, HEAD_DIM, STATE))
        conv_s.append(xbc[T_PROMPT:].reshape(DEC_BATCH, DEC_SEQ, -1)[:, -(CONV_W - 1):])
        v_s.append(vn_s.reshape(DEC_BATCH, DEC_SEQ, D_CM))
    out = _rms(h, g_final, F32)
    return (out[:T_PROMPT].reshape(BATCH, SEQ, D_MODEL), out[T_PROMPT:].reshape(DEC_BATCH, DEC_SEQ, D_MODEL),
            jnp.stack(ssm_p), jnp.stack(conv_p), jnp.stack(ssm_s), jnp.stack(conv_s), jnp.stack(v_s))
```

```python
import functools

import jax
import jax.numpy as jnp
import numpy as np
from jax import lax
from jax.experimental import pallas as pl
from jax.experimental.pallas import tpu as pltpu

F32 = jnp.float32
BF16 = jnp.bfloat16

D_MODEL = 2048
BATCH = 4
SEQ = 2048
DEPTH = 4
DEC_BATCH = 128
DEC_SEQ = 4
D_SSD = 2048
HEAD_DIM = 64
HEADS = 32
GROUPS = 4
HEADS_PER_GROUP = HEADS // GROUPS
GROUP_W = D_SSD // GROUPS
STATE = 128
CONV_W = 4
BC_W = 2 * GROUPS * STATE
D_CM = 2048
CM_GROUPS = 16
D_FF = 5632
D_PLE = 256
EPS = 1e-6

T_PROMPT = BATCH * SEQ
T_SAMPLE = DEC_BATCH * DEC_SEQ
T_ALL = T_PROMPT + T_SAMPLE
TILE = 128
LANES = 128
N_MAIN = 4 * 2048 + BC_W

VMEM_LIMIT = 56 * 1024 * 1024


def _cparams(n_axes):
    return pltpu.CompilerParams(dimension_semantics=("arbitrary",) * n_axes,
                                vmem_limit_bytes=VMEM_LIMIT)


def _rms_cast_kernel(x_ref, g_ref, o_ref):
    x = x_ref[...]
    y = x * lax.rsqrt(jnp.mean(x * x, axis=-1, keepdims=True) + EPS)
    o_ref[...] = (y * g_ref[...]).astype(o_ref.dtype)


def _rms(x, g, out_dtype, tm=544):
    m, d = x.shape
    return pl.pallas_call(
        _rms_cast_kernel,
        grid=(m // tm,),
        in_specs=[pl.BlockSpec((tm, d), lambda i: (i, 0)),
                  pl.BlockSpec((1, d), lambda i: (0, 0))],
        out_specs=pl.BlockSpec((tm, d), lambda i: (i, 0)),
        out_shape=jax.ShapeDtypeStruct((m, d), out_dtype),
        compiler_params=_cparams(1),
        name="rms_" + jnp.dtype(out_dtype).name,
    )(x, g.reshape(1, d))


IN_TN = 1024
IN_ALIGNED_TILES = (2 * D_SSD + BC_W) // IN_TN
IN_SHIFT = HEADS
IN_TILES = IN_ALIGNED_TILES + 2 * D_CM // IN_TN


def _in_proj_kernel(x_ref, w_ref, wt_ref, o_ref, wb_ref):
    j = pl.program_id(0)
    first = pl.program_id(1) == 0

    @pl.when(first & (j < IN_ALIGNED_TILES))
    def _():
        wb_ref[...] = w_ref[...].astype(BF16)

    @pl.when(first & (j >= IN_ALIGNED_TILES))
    def _():
        wb_ref[:, :IN_TN - IN_SHIFT] = w_ref[:, IN_SHIFT:].astype(BF16)
        wb_ref[:, IN_TN - IN_SHIFT:] = wt_ref[:, :IN_SHIFT].astype(BF16)

    o_ref[...] = jnp.dot(x_ref[...], wb_ref[...], preferred_element_type=F32)


def _in_proj(x, w_in, layer, tm):
    m, k = x.shape
    per128 = IN_TN // LANES

    def out_col(j):
        return jnp.where(j < IN_ALIGNED_TILES - 1, j, jnp.where(j == IN_ALIGNED_TILES - 1, IN_TILES - 1, j - 1))

    return pl.pallas_call(
        _in_proj_kernel,
        grid=(IN_TILES, m // tm),
        in_specs=[pl.BlockSpec((tm, k), lambda j, i: (i, 0)),
                  pl.BlockSpec((None, k, IN_TN), lambda j, i: (layer, 0, j)),
                  pl.BlockSpec((None, k, LANES),
                               lambda j, i: (layer, 0, per128 * (jnp.maximum(j, IN_ALIGNED_TILES - 1) + 1)))],
        out_specs=pl.BlockSpec((tm, IN_TN), lambda j, i: (i, out_col(j))),
        out_shape=jax.ShapeDtypeStruct((m, IN_TILES * IN_TN), F32),
        scratch_shapes=[pltpu.VMEM((k, IN_TN), BF16)],
        compiler_params=_cparams(2),
        name="in_proj",
    )(x, w_in, w_in)


def _in_proj_dt_kernel(x_ref, w_ref, o_ref):
    o_ref[...] = jnp.dot(x_ref[...], w_ref[...].astype(BF16), preferred_element_type=F32)


def _in_proj_dt(x, w_in, layer, tm):
    m, k = x.shape
    blk = (2 * D_SSD + BC_W) // LANES
    return pl.pallas_call(
        _in_proj_dt_kernel,
        grid=(m // tm,),
        in_specs=[pl.BlockSpec((tm, k), lambda i: (i, 0)),
                  pl.BlockSpec((None, k, LANES), lambda i: (layer, 0, blk))],
        out_specs=pl.BlockSpec((tm, LANES), lambda i: (i, 0)),
        out_shape=jax.ShapeDtypeStruct((m, LANES), F32),
        compiler_params=_cparams(1),
        name="in_proj_dt",
    )(x, w_in)


def _mm_res_kernel(x_ref, w_ref, h_ref, o_ref, wb_ref):
    @pl.when(pl.program_id(1) == 0)
    def _():
        wb_ref[...] = w_ref[...].astype(BF16)

    o_ref[...] = h_ref[...] + jnp.dot(x_ref[...], wb_ref[...], preferred_element_type=F32)


def _mm_res(x, w, layer, h, tm, tn, name):
    m, k = x.shape
    n = w.shape[2]
    return pl.pallas_call(
        _mm_res_kernel,
        grid=(n // tn, m // tm),
        in_specs=[pl.BlockSpec((tm, k), lambda j, i: (i, 0)),
                  pl.BlockSpec((None, k, tn), lambda j, i: (layer, 0, j)),
                  pl.BlockSpec((tm, tn), lambda j, i: (i, j))],
        out_specs=pl.BlockSpec((tm, tn), lambda j, i: (i, j)),
        out_shape=jax.ShapeDtypeStruct((m, n), F32),
        scratch_shapes=[pltpu.VMEM((k, tn), BF16)],
        compiler_params=_cparams(2),
        name=name,
    )(x, w, h)


def _ffn_up_kernel(x_ref, wg_ref, wu_ref, o_ref, wgb_ref, wub_ref):
    @pl.when(pl.program_id(1) == 0)
    def _():
        wgb_ref[...] = wg_ref[...].astype(BF16)
        wub_ref[...] = wu_ref[...].astype(BF16)

    x = x_ref[...]
    gate = jnp.dot(x, wgb_ref[...], preferred_element_type=F32)
    up = jnp.dot(x, wub_ref[...], preferred_element_type=F32)
    o_ref[...] = (jax.nn.silu(gate) * up).astype(o_ref.dtype)


def _ffn_up(x, wg, wu, layer, tm, tn):
    m, k = x.shape
    n = wg.shape[2]
    return pl.pallas_call(
        _ffn_up_kernel,
        grid=(n // tn, m // tm),
        in_specs=[pl.BlockSpec((tm, k), lambda j, i: (i, 0)),
                  pl.BlockSpec((None, k, tn), lambda j, i: (layer, 0, j)),
                  pl.BlockSpec((None, k, tn), lambda j, i: (layer, 0, j))],
        out_specs=pl.BlockSpec((tm, tn), lambda j, i: (i, j)),
        out_shape=jax.ShapeDtypeStruct((m, n), BF16),
        scratch_shapes=[pltpu.VMEM((k, tn), BF16), pltpu.VMEM((k, tn), BF16)],
        compiler_params=_cparams(2),
        name="ffn_up",
    )(x, wg, wu)


def _ple_kernel(n_ref, p_ref, h_ref, wpg_ref, wple_ref, o_ref, wpgb_ref, wpleb_ref):
    @pl.when(pl.program_id(1) == 0)
    def _():
        wpgb_ref[...] = wpg_ref[...].astype(BF16)
        wpleb_ref[...] = wple_ref[...].astype(BF16)

    gate = jax.nn.sigmoid(jnp.dot(n_ref[...], wpgb_ref[...], preferred_element_type=F32))
    emb = jnp.dot(p_ref[...].astype(BF16), wpleb_ref[...], preferred_element_type=F32)
    o_ref[...] = h_ref[...] + emb * gate


def _ple(n, p, h, wpg, wple, layer, tm, tn):
    m, k = n.shape
    kp = p.shape[2]
    nn = wpg.shape[2]
    return pl.pallas_call(
        _ple_kernel,
        grid=(nn // tn, m // tm),
        in_specs=[pl.BlockSpec((tm, k), lambda j, i: (i, 0)),
                  pl.BlockSpec((None, tm, kp), lambda j, i: (layer, i, 0)),
                  pl.BlockSpec((tm, tn), lambda j, i: (i, j)),
                  pl.BlockSpec((None, k, tn), lambda j, i: (layer, 0, j)),
                  pl.BlockSpec((None, kp, tn), lambda j, i: (layer, 0, j))],
        out_specs=pl.BlockSpec((tm, tn), lambda j, i: (i, j)),
        out_shape=jax.ShapeDtypeStruct((m, nn), F32),
        scratch_shapes=[pltpu.VMEM((k, tn), BF16), pltpu.VMEM((kp, tn), BF16)],
        compiler_params=_cparams(2),
        name="ple",
    )(n, p, h, wpg, wple)


def _softplus(x):
    return jnp.maximum(x, 0.0) + jnp.log1p(jnp.exp(-jnp.abs(x)))


def _expand_heads(v, e2_ref):
    lane = lax.broadcasted_iota(jnp.int32, v.shape, 1)
    v = jnp.where(lane < HEADS, v, 0.0)
    hi = v.astype(BF16)
    lo = (v - hi.astype(F32)).astype(BF16)
    return jnp.dot(jnp.concatenate([hi, lo], axis=1), e2_ref[...], preferred_element_type=F32)


def _mixer_kernel(*refs, mode):
    prompt = mode == "prompt"
    prep = mode == "sample_prep"
    it = iter(refs)
    z_ref, x_ref, u_ref, v_ref, bc_ref, dt_ref = (next(it) for _ in range(6))
    if not prompt:
        hx_refs = [next(it) for _ in range(3)]
        hbc_refs = [next(it) for _ in range(3)]
    if mode == "sample_final":
        yofft_ref = next(it)
        next(it)
    (cwx_ref, cwbc_ref, cbx_ref, cbbc_ref, dtb_ref, alog_ref, dsk_ref, gssd_ref, gcm_ref,
     e2_ref, ws_ref, bexp_ref) = (next(it) for _ in range(12))
    if prompt:
        y_ref, ssm_ref = next(it), next(it)
        st_ref, extx_ref, extbc_ref, xs_ref = (next(it) for _ in range(4))
    elif prep:
        xst_ref, bm_ref, cmt_ref, ea_ref = (next(it) for _ in range(4))
        extx_ref, extbc_ref, xs_ref = (next(it) for _ in range(3))
    else:
        y_ref, vn_ref = next(it), next(it)
        extx_ref, extbc_ref, xs_ref = (next(it) for _ in range(3))

    row = lax.broadcasted_iota(jnp.int32, (TILE, LANES), 0)
    col = lax.broadcasted_iota(jnp.int32, (TILE, LANES), 1)
    if prompt:
        pair_mask = col <= row
    else:
        pair_mask = (col <= row) & ((col >> 2) == (row >> 2))

    if prompt:
        @pl.when(pl.program_id(1) == 0)
        def _():
            st_ref[...] = jnp.zeros_like(st_ref)
            extx_ref[0:8, :] = jnp.zeros((8, D_SSD), F32)
            extbc_ref[0:8, :] = jnp.zeros((8, BC_W), F32)

    def conv(src_ref, ext_ref, w_ref, b_ref, hist_refs, width):
        if not prompt:
            ext_ref[0:8, :] = jnp.zeros((8, width), F32)
        ext_ref[8:8 + TILE, :] = src_ref[...]
        acc = b_ref[...] + w_ref[CONV_W - 1:CONV_W, :] * ext_ref[8:8 + TILE, :]
        rw = lax.broadcasted_iota(jnp.int32, (TILE, width), 0)
        for j in range(1, CONV_W):
            shifted = ext_ref[8 - j:8 - j + TILE, :]
            if not prompt:
                shifted = jnp.where((rw & (DEC_SEQ - 1)) >= j, shifted, hist_refs[j - 1][...])
            acc = acc + w_ref[CONV_W - 1 - j:CONV_W - j, :] * shifted
        if prompt:
            ext_ref[0:8, :] = ext_ref[TILE:TILE + 8, :]
        return jax.nn.silu(acc)

    xs_ref[...] = conv(x_ref, extx_ref, cwx_ref, cbx_ref, None if prompt else hx_refs, D_SSD)
    bcs = conv(bc_ref, extbc_ref, cwbc_ref, cbbc_ref, None if prompt else hbc_refs, BC_W)
    bm = [bcs[:, g * STATE:(g + 1) * STATE] for g in range(GROUPS)]
    cm = [bcs[:, (GROUPS + g) * STATE:(GROUPS + g + 1) * STATE] for g in range(GROUPS)]

    dt = _softplus(dt_ref[...] + dtb_ref[...])
    dta = dt * (-jnp.exp(alog_ref[...]))
    acum = dta
    if prompt:
        for s in (1, 2, 4, 8, 16, 32, 64):
            acum = acum + jnp.where(row >= s, pltpu.roll(acum, s, axis=0), 0.0)
        alast = acum[TILE - 1:TILE, :]
    else:
        pos = row & (DEC_SEQ - 1)
        for s in (1, 2):
            acum = acum + jnp.where(pos >= s, pltpu.roll(acum, s, axis=0), 0.0)
        alast = jnp.where(pos == DEC_SEQ - 1, acum, 0.0)
        alast = alast + jnp.where(pos == DEC_SEQ - 2, pltpu.roll(alast, TILE - 1, axis=0), 0.0)
        alast = alast + jnp.where(pos < DEC_SEQ - 2, pltpu.roll(alast, TILE - 2, axis=0), 0.0)
    ea = jnp.exp(acum)
    dsc = dt * jnp.exp(alast - acum)
    dsc_e = _expand_heads(dsc, e2_ref)

    if prep:
        xst_ref[...] = (xs_ref[...] * dsc_e).T.astype(BF16)
        bm_ref[...] = bcs[:, :GROUPS * STATE]
        for g in range(GROUPS):
            cmt_ref[g * STATE:(g + 1) * STATE, :] = cm[g].T
        ea_ref[...] = ea
        return

    dt_e = _expand_heads(dt, e2_ref)
    ea_e = _expand_heads(ea, e2_ref)
    acum_t = acum.T

    cb = [lax.dot_general(cm[g].astype(BF16), bm[g].astype(BF16), (((1,), (1,)), ((), ())),
                          preferred_element_type=F32) for g in range(GROUPS)]
    ydiag = []
    for j in range(HEADS // 2):
        w_pair = []
        for h in (2 * j, 2 * j + 1):
            seg = acum[:, h:h + 1] - acum_t[h:h + 1, :]
            dec = jnp.exp(jnp.where(pair_mask, seg, -jnp.inf))
            w_pair.append((cb[h // HEADS_PER_GROUP] * dec).astype(BF16))
        xdt = (xs_ref[:, j * LANES:(j + 1) * LANES] * dt_e[:, j * LANES:(j + 1) * LANES]).astype(BF16)
        lo_half = col < HEAD_DIM
        rhs = jnp.concatenate([jnp.where(lo_half, xdt, jnp.zeros_like(xdt)),
                               jnp.where(lo_half, jnp.zeros_like(xdt), xdt)], axis=0)
        ydiag.append(jnp.dot(jnp.concatenate(w_pair, axis=1), rhs, preferred_element_type=F32))

    if prompt:
        yoff = [jnp.dot(cm[g].astype(BF16), st_ref[:, g * GROUP_W:(g + 1) * GROUP_W].astype(BF16),
                        preferred_element_type=F32) for g in range(GROUPS)]
        for g in range(GROUPS):
            sl = slice(g * GROUP_W, (g + 1) * GROUP_W)
            xsc = (xs_ref[:, sl] * dsc_e[:, sl]).astype(BF16)
            upd = jnp.dot(bm[g].T.astype(BF16), xsc, preferred_element_type=F32)
            st_ref[:, sl] = st_ref[:, sl] * ea_e[TILE - 1:TILE, sl] + upd

        @pl.when(pl.program_id(1) == pl.num_programs(1) - 1)
        def _():
            ssm_ref[0] = st_ref[...].T
    else:
        yoff_all = yofft_ref[...].T
        yoff = [yoff_all[:, g * GROUP_W:(g + 1) * GROUP_W] for g in range(GROUPS)]

    for g in range(GROUPS):
        sl = slice(g * GROUP_W, (g + 1) * GROUP_W)
        yd = jnp.concatenate(ydiag[g * 4:(g + 1) * 4], axis=1)
        y = yd + yoff[g] * ea_e[:, sl] + dsk_ref[:, sl] * xs_ref[:, sl]
        y = y * jax.nn.silu(z_ref[:, sl])
        y = y * lax.rsqrt(jnp.mean(y * y, axis=-1, keepdims=True) + EPS)
        y_ref[:, sl] = (y * gssd_ref[:, sl]).astype(y_ref.dtype)

    gv = jax.nn.gelu(v_ref[...])
    vn = gv * lax.rsqrt(jnp.mean(gv * gv, axis=-1, keepdims=True) + EPS) * gcm_ref[...]
    if not prompt:
        vn_ref[...] = vn
    for g in range(CM_GROUPS):
        sl = slice(g * LANES, (g + 1) * LANES)
        wm = jnp.where(pair_mask, ws_ref[g], 0.0).astype(BF16)
        s = jnp.dot(wm, vn[:, sl].astype(BF16), preferred_element_type=F32) + bexp_ref[:, sl]
        y_ref[:, D_SSD + g * LANES:D_SSD + (g + 1) * LANES] = (
            jax.nn.gelu(u_ref[:, sl]) * s).astype(y_ref.dtype)


def _state_kernel(dec_ref, st_ref, xst_ref, bm_ref, cmt_ref, new_ref, yt_ref):
    b = pl.program_id(0)
    per_tile = TILE // DEC_SEQ
    bl = b % per_tile

    @pl.when(bl == 0)
    def _():
        yt_ref[...] = jnp.zeros_like(yt_ref)

    row = lax.broadcasted_iota(jnp.int32, (TILE, LANES), 0)
    col = lax.broadcasted_iota(jnp.int32, (TILE, LANES), 1)
    mine_rows = (row >> 2) == bl
    mine_cols = (col >> 2) == bl
    for g in range(GROUPS):
        rs = slice(g * GROUP_W, (g + 1) * GROUP_W)
        sg = st_ref[0, rs, :]
        ctm = jnp.where(mine_cols, cmt_ref[g * STATE:(g + 1) * STATE, :], 0.0).astype(BF16)
        yt_ref[rs, :] += jnp.dot(sg.astype(BF16), ctm, preferred_element_type=F32)
        bmk = jnp.where(mine_rows, bm_ref[:, g * STATE:(g + 1) * STATE], 0.0).astype(BF16)
        upd = jnp.dot(xst_ref[rs, :], bmk, preferred_element_type=F32)
        for hh in range(HEADS_PER_GROUP):
            h = g * HEADS_PER_GROUP + hh
            hs = slice(hh * HEAD_DIM, (hh + 1) * HEAD_DIM)
            new_ref[0, g * GROUP_W + hh * HEAD_DIM:g * GROUP_W + (hh + 1) * HEAD_DIM, :] = (
                sg[hs, :] * dec_ref[b, h] + upd[hs, :])


def _const_spec(shape):
    nd = len(shape)
    return pl.BlockSpec(shape, lambda *idx: (0,) * nd)


def _param_specs():
    return [_const_spec((CONV_W, D_SSD)), _const_spec((CONV_W, BC_W)), _const_spec((1, D_SSD)),
            _const_spec((1, BC_W)), _const_spec((1, LANES)), _const_spec((1, LANES)),
            _const_spec((1, D_SSD)), _const_spec((1, D_SSD)), _const_spec((1, D_CM)),
            _const_spec((2 * LANES, D_SSD)), _const_spec((CM_GROUPS, TILE, TILE)),
            _const_spec((TILE, D_CM))]


def _mix_prompt(proj, dtraw, params):
    tiles = SEQ // TILE

    def rt(b, c):
        return b * tiles + c

    in_specs = [pl.BlockSpec((TILE, 2048), lambda b, c: (rt(b, c), 0)),
                pl.BlockSpec((TILE, 2048), lambda b, c: (rt(b, c), 1)),
                pl.BlockSpec((TILE, 2048), lambda b, c: (rt(b, c), 2)),
                pl.BlockSpec((TILE, 2048), lambda b, c: (rt(b, c), 3)),
                pl.BlockSpec((TILE, BC_W), lambda b, c: (rt(b, c), 8)),
                pl.BlockSpec((TILE, LANES), lambda b, c: (rt(b, c), 0))]
    in_specs += _param_specs()
    return pl.pallas_call(
        functools.partial(_mixer_kernel, mode="prompt"),
        grid=(BATCH, tiles),
        in_specs=in_specs,
        out_specs=[pl.BlockSpec((TILE, 2 * D_SSD), lambda b, c: (rt(b, c), 0)),
                   pl.BlockSpec((1, D_SSD, STATE), lambda b, c: (b, 0, 0))],
        out_shape=[jax.ShapeDtypeStruct((T_ALL, 2 * D_SSD), BF16),
                   jax.ShapeDtypeStruct((BATCH, D_SSD, STATE), F32)],
        scratch_shapes=[pltpu.VMEM((STATE, D_SSD), F32),
                        pltpu.VMEM((TILE + 8, D_SSD), F32),
                        pltpu.VMEM((TILE + 8, BC_W), F32),
                        pltpu.VMEM((TILE, D_SSD), F32)],
        compiler_params=_cparams(2),
        name="mix_prompt",
    )(proj, proj, proj, proj, proj, dtraw, *params)


def _sample_in_specs():
    base = T_PROMPT // TILE
    return [pl.BlockSpec((TILE, 2048), lambda t: (base + t, 0)),
            pl.BlockSpec((TILE, 2048), lambda t: (base + t, 1)),
            pl.BlockSpec((TILE, 2048), lambda t: (base + t, 2)),
            pl.BlockSpec((TILE, 2048), lambda t: (base + t, 3)),
            pl.BlockSpec((TILE, BC_W), lambda t: (base + t, 8)),
            pl.BlockSpec((TILE, LANES), lambda t: (base + t, 0))]


def _hist_specs():
    return ([pl.BlockSpec((TILE, D_SSD), lambda t: (t, 0))] * 3
            + [pl.BlockSpec((TILE, BC_W), lambda t: (t, 0))] * 3)


def _sample_scratch():
    return [pltpu.VMEM((TILE + 8, D_SSD), F32), pltpu.VMEM((TILE + 8, BC_W), F32),
            pltpu.VMEM((TILE, D_SSD), F32)]


def _mix_sample_prep(proj, dtraw, hists, params):
    nt = T_SAMPLE // TILE
    return pl.pallas_call(
        functools.partial(_mixer_kernel, mode="sample_prep"),
        grid=(nt,),
        in_specs=_sample_in_specs() + _hist_specs() + _param_specs(),
        out_specs=[pl.BlockSpec((D_SSD, TILE), lambda t: (0, t)),
                   pl.BlockSpec((TILE, GROUPS * STATE), lambda t: (t, 0)),
                   pl.BlockSpec((GROUPS * STATE, TILE), lambda t: (0, t)),
                   pl.BlockSpec((TILE, LANES), lambda t: (t, 0))],
        out_shape=[jax.ShapeDtypeStruct((D_SSD, T_SAMPLE), BF16),
                   jax.ShapeDtypeStruct((T_SAMPLE, GROUPS * STATE), F32),
                   jax.ShapeDtypeStruct((GROUPS * STATE, T_SAMPLE), F32),
                   jax.ShapeDtypeStruct((T_SAMPLE, LANES), F32)],
        scratch_shapes=_sample_scratch(),
        compiler_params=_cparams(1),
        name="mix_sample_prep",
    )(proj, proj, proj, proj, proj, dtraw, *hists, *params)


def _mix_sample_final(proj, dtraw, hists, yofft, ybuf, params):
    nt = T_SAMPLE // TILE
    base = T_PROMPT // TILE
    n_in = 6 + 6 + 1
    return pl.pallas_call(
        functools.partial(_mixer_kernel, mode="sample_final"),
        grid=(nt,),
        in_specs=(_sample_in_specs() + _hist_specs()
                  + [pl.BlockSpec((D_SSD, TILE), lambda t: (0, t)), pl.BlockSpec(memory_space=pl.ANY)]
                  + _param_specs()),
        out_specs=[pl.BlockSpec((TILE, 2 * D_SSD), lambda t: (base + t, 0)),
                   pl.BlockSpec((TILE, D_CM), lambda t: (t, 0))],
        out_shape=[jax.ShapeDtypeStruct((T_ALL, 2 * D_SSD), BF16),
                   jax.ShapeDtypeStruct((T_SAMPLE, D_CM), F32)],
        input_output_aliases={n_in: 0},
        scratch_shapes=_sample_scratch(),
        compiler_params=_cparams(1),
        name="mix_sample_final",
    )(proj, proj, proj, proj, proj, dtraw, *hists, yofft, ybuf, *params)


def _state_kernel_aliased(dec_ref, st_ref, xst_ref, bm_ref, cmt_ref, prev_ref, new_ref, yt_ref):
    del prev_ref
    _state_kernel(dec_ref, st_ref, xst_ref, bm_ref, cmt_ref, new_ref, yt_ref)


def _sample_state(dec, state, layer, xst, bm, cmt, new_all):
    per_tile = TILE // DEC_SEQ
    in_specs = [pl.BlockSpec(memory_space=pltpu.SMEM),
                pl.BlockSpec((None, 1, D_SSD, STATE), lambda b: (layer, b, 0, 0)),
                pl.BlockSpec((D_SSD, TILE), lambda b: (0, b // per_tile)),
                pl.BlockSpec((TILE, GROUPS * STATE), lambda b: (b // per_tile, 0)),
                pl.BlockSpec((GROUPS * STATE, TILE), lambda b: (0, b // per_tile))]
    args = [dec, state, xst, bm, cmt]
    aliases = {}
    body = _state_kernel
    if new_all is not None:
        in_specs.append(pl.BlockSpec(memory_space=pl.ANY))
        args.append(new_all)
        aliases = {5: 0}
        body = _state_kernel_aliased
    return pl.pallas_call(
        body,
        grid=(DEC_BATCH,),
        in_specs=in_specs,
        out_specs=[pl.BlockSpec((None, 1, D_SSD, STATE), lambda b: (layer, b, 0, 0)),
                   pl.BlockSpec((D_SSD, TILE), lambda b: (0, b // per_tile))],
        out_shape=[jax.ShapeDtypeStruct((DEPTH, DEC_BATCH, D_SSD, STATE), F32),
                   jax.ShapeDtypeStruct((D_SSD, T_SAMPLE), F32)],
        input_output_aliases=aliases,
        compiler_params=_cparams(1),
        name="sample_state",
    )(*args)


def _expansion_matrix():
    e = np.zeros((2 * LANES, D_SSD), np.float32)
    for h in range(HEADS):
        e[h, h * HEAD_DIM:(h + 1) * HEAD_DIM] = 1.0
        e[LANES + h, h * HEAD_DIM:(h + 1) * HEAD_DIM] = 1.0
    return jnp.asarray(e, BF16)


def _pad_lanes(v):
    return jnp.pad(v, (0, LANES - v.shape[0])).reshape(1, LANES)


def kernel(x_prompt, x_sample, state_ssm, state_conv, p_prompt, p_sample, g_mix, w_in, conv_w, conv_b, dt_bias, a_log, d_skip, g_ssd, g_cm, w_s, b_s, w_out, g_ffn, w_gate, w_up, w_down, g_pg, w_pg, w_ple, g_final):
    h = jnp.concatenate([x_prompt.reshape(T_PROMPT, D_MODEL), x_sample.reshape(T_SAMPLE, D_MODEL)], axis=0)
    p_all = jnp.concatenate([p_prompt.reshape(DEPTH, T_PROMPT, D_PLE), p_sample.reshape(DEPTH, T_SAMPLE, D_PLE)], axis=1)
    state_all = state_ssm.reshape(DEPTH, DEC_BATCH, D_SSD, STATE)
    e2 = _expansion_matrix()
    ssm_p, conv_p, conv_s, v_s = [], [], [], []
    ssm_s_all = None
    for i in range(DEPTH):
        params_common = (conv_w[i][:, :D_SSD], conv_w[i][:, D_SSD:], conv_b[i][:D_SSD].reshape(1, -1),
                         conv_b[i][D_SSD:].reshape(1, -1), _pad_lanes(dt_bias[i]), _pad_lanes(a_log[i]),
                         jnp.repeat(d_skip[i], HEAD_DIM).reshape(1, -1), g_ssd[i].reshape(1, -1),
                         g_cm[i].reshape(1, -1), e2)
        params_p = params_common + (w_s[i], jnp.repeat(b_s[i].T, LANES, axis=1))
        reps = TILE // DEC_SEQ
        params_s = params_common + (jnp.tile(w_s[i][:, :DEC_SEQ, :DEC_SEQ], (1, reps, reps)),
                                    jnp.tile(jnp.repeat(b_s[i][:, :DEC_SEQ].T, LANES, axis=1), (reps, 1)))
        sc = state_conv[i]
        hists_x, hists_bc = [], []
        for j in range(1, CONV_W):
            hj = jnp.pad(sc[:, CONV_W - 1 - j:, :], ((0, 0), (0, DEC_SEQ - j), (0, 0))).reshape(T_SAMPLE, -1)
            hists_x.append(hj[:, :D_SSD])
            hists_bc.append(hj[:, D_SSD:])
        hists = hists_x + hists_bc

        n = _rms(h, g_mix[i], BF16)
        proj = _in_proj(n, w_in, i, 1088)
        dtraw = _in_proj_dt(n, w_in, i, 1088)

        ybuf, ssm_new_p = _mix_prompt(proj, dtraw, params_p)
        xst, bmat, cmt, ea = _mix_sample_prep(proj, dtraw, hists, params_s)
        dec = ea[DEC_SEQ - 1::DEC_SEQ, :HEADS]
        ssm_s_all, yofft = _sample_state(dec, state_all, i, xst, bmat, cmt, ssm_s_all)
        y, vn_s = _mix_sample_final(proj, dtraw, hists, yofft, ybuf, params_s)

        h = _mm_res(y, w_out, i, h, 544, 512, "out_proj")
        f = _rms(h, g_ffn[i], BF16)
        act = _ffn_up(f, w_gate, w_up, i, 1088, 512)
        h = _mm_res(act, w_down, i, h, 544, 512, "ffn_down")
        npg = _rms(h, g_pg[i], BF16)
        h = _ple(npg, p_all, h, w_pg, w_ple, i, 1088, 512)

        def xbc_rows(r0, nrows):
            return jnp.concatenate([lax.slice(proj, (r0, D_SSD), (r0 + nrows, 2 * D_SSD)),
                                    lax.slice(proj, (r0, 4 * 2048), (r0 + nrows, 4 * 2048 + BC_W))], axis=1)

        ssm_p.append(ssm_new_p.reshape(BATCH, HEADS, HEAD_DIM, STATE))
        conv_p.append(jnp.stack([xbc_rows((b + 1) * SEQ - (CONV_W - 1), CONV_W - 1) for b in range(BATCH)]))
        conv_s.append(xbc_rows(T_PROMPT, T_SAMPLE).reshape(DEC_BATCH, DEC_SEQ, -1)[:, -(CONV_W - 1):])
        v_s.append(vn_s.reshape(DEC_BATCH, DEC_SEQ, D_CM))
    out = _rms(h, g_final, F32)
    return (out[:T_PROMPT].reshape(BATCH, SEQ, D_MODEL), out[T_PROMPT:].reshape(DEC_BATCH, DEC_SEQ, D_MODEL),
            jnp.stack(ssm_p), jnp.stack(conv_p),
            ssm_s_all.reshape(DEPTH, DEC_BATCH, HEADS, HEAD_DIM, STATE), jnp.stack(conv_s), jnp.stack(v_s))
```

```python
import functools

import jax
import jax.numpy as jnp
import numpy as np
from jax import lax
from jax.experimental import pallas as pl
from jax.experimental.pallas import tpu as pltpu

F32 = jnp.float32
BF16 = jnp.bfloat16

D_MODEL = 2048
BATCH = 4
SEQ = 2048
DEPTH = 4
DEC_BATCH = 128
DEC_SEQ = 4
D_SSD = 2048
HEAD_DIM = 64
HEADS = 32
GROUPS = 4
HEADS_PER_GROUP = HEADS // GROUPS
GROUP_W = D_SSD // GROUPS
STATE = 128
CONV_W = 4
BC_W = 2 * GROUPS * STATE
D_CM = 2048
CM_GROUPS = 16
D_FF = 5632
D_PLE = 256
EPS = 1e-6

T_PROMPT = BATCH * SEQ
T_SAMPLE = DEC_BATCH * DEC_SEQ
T_ALL = T_PROMPT + T_SAMPLE
TILE = 128
LANES = 128

VMEM_LIMIT = 56 * 1024 * 1024


def _cparams(n_axes):
    return pltpu.CompilerParams(dimension_semantics=("arbitrary",) * n_axes,
                                vmem_limit_bytes=VMEM_LIMIT)


def _rms_cast_kernel(x_ref, g_ref, o_ref):
    x = x_ref[...]
    y = x * lax.rsqrt(jnp.mean(x * x, axis=-1, keepdims=True) + EPS)
    o_ref[...] = (y * g_ref[...]).astype(o_ref.dtype)


def _rms(x, g, out_dtype, tm=544, row0=0, rows=None):
    d = x.shape[1]
    m = x.shape[0] if rows is None else rows
    blk0 = row0 // tm
    return pl.pallas_call(
        _rms_cast_kernel,
        grid=(m // tm,),
        in_specs=[pl.BlockSpec((tm, d), lambda i: (blk0 + i, 0)),
                  pl.BlockSpec((1, d), lambda i: (0, 0))],
        out_specs=pl.BlockSpec((tm, d), lambda i: (i, 0)),
        out_shape=jax.ShapeDtypeStruct((m, d), out_dtype),
        compiler_params=_cparams(1),
        name="rms_" + jnp.dtype(out_dtype).name,
    )(x, g.reshape(1, d))


IN_TN = 1024
IN_ALIGNED_TILES = (2 * D_SSD + BC_W) // IN_TN
IN_SHIFT = HEADS
IN_TILES = IN_ALIGNED_TILES + 2 * D_CM // IN_TN


_NT = (((1,), (1,)), ((), ()))


def _in_proj_kernel(x_ref, w_ref, wnext_ref, o_ref, wb_ref):
    j = pl.program_id(0)
    first = pl.program_id(1) == 0

    @pl.when(first & (j < IN_ALIGNED_TILES))
    def _():
        wb_ref[...] = w_ref[...].astype(BF16)

    @pl.when(first & (j >= IN_ALIGNED_TILES))
    def _():
        wb_ref[:IN_TN - IN_SHIFT, :] = w_ref[IN_SHIFT:, :].astype(BF16)
        wb_ref[IN_TN - IN_SHIFT:, :] = wnext_ref[...].astype(BF16)

    def project():
        return lax.dot_general(x_ref[...], wb_ref[...], _NT, preferred_element_type=F32)

    @pl.when(j < D_SSD // IN_TN)
    def _():
        o_ref[...] = jax.nn.silu(project())

    @pl.when((j >= D_SSD // IN_TN) & (j < IN_ALIGNED_TILES))
    def _():
        o_ref[...] = project()

    @pl.when(j >= IN_ALIGNED_TILES)
    def _():
        o_ref[...] = jax.nn.gelu(project())


def _in_proj(x, w_in_t, layer, tm):
    m, k = x.shape
    per_shift = IN_TN // IN_SHIFT

    def out_col(j):
        return jnp.where(j < IN_ALIGNED_TILES - 1, j, jnp.where(j == IN_ALIGNED_TILES - 1, IN_TILES - 1, j - 1))

    return pl.pallas_call(
        _in_proj_kernel,
        grid=(IN_TILES, m // tm),
        in_specs=[pl.BlockSpec((tm, k), lambda j, i: (i, 0)),
                  pl.BlockSpec((None, IN_TN, k), lambda j, i: (layer, j, 0)),
                  pl.BlockSpec((None, IN_SHIFT, k),
                               lambda j, i: (layer, per_shift * (jnp.maximum(j, IN_ALIGNED_TILES - 1) + 1), 0))],
        out_specs=pl.BlockSpec((tm, IN_TN), lambda j, i: (i, out_col(j))),
        out_shape=jax.ShapeDtypeStruct((m, IN_TILES * IN_TN), F32),
        scratch_shapes=[pltpu.VMEM((IN_TN, k), BF16)],
        compiler_params=_cparams(2),
        name="in_proj",
    )(x, w_in_t, w_in_t)


def _in_proj_dt_kernel(x_ref, w_ref, o_ref):
    o_ref[...] = lax.dot_general(x_ref[...], w_ref[...].astype(BF16), _NT, preferred_element_type=F32)


def _in_proj_dt(x, w_in_t, layer, tm):
    m, k = x.shape
    blk = (2 * D_SSD + BC_W) // LANES
    return pl.pallas_call(
        _in_proj_dt_kernel,
        grid=(m // tm,),
        in_specs=[pl.BlockSpec((tm, k), lambda i: (i, 0)),
                  pl.BlockSpec((None, LANES, k), lambda i: (layer, blk, 0))],
        out_specs=pl.BlockSpec((tm, LANES), lambda i: (i, 0)),
        out_shape=jax.ShapeDtypeStruct((m, LANES), F32),
        compiler_params=_cparams(1),
        name="in_proj_dt",
    )(x, w_in_t)


def _mm_res_kernel(x_ref, w_ref, h_ref, o_ref, wb_ref):
    @pl.when(pl.program_id(1) == 0)
    def _():
        wb_ref[...] = w_ref[...].astype(BF16)

    o_ref[...] = h_ref[...] + jnp.dot(x_ref[...], wb_ref[...], preferred_element_type=F32)


def _mm_res(x, w, layer, h, tm, tn, name, weight_buffers=2):
    m, k = x.shape
    n = w.shape[2]
    return pl.pallas_call(
        _mm_res_kernel,
        grid=(n // tn, m // tm),
        in_specs=[pl.BlockSpec((tm, k), lambda j, i: (i, 0)),
                  pl.BlockSpec((None, k, tn), lambda j, i: (layer, 0, j),
                               pipeline_mode=pl.Buffered(weight_buffers)),
                  pl.BlockSpec((tm, tn), lambda j, i: (i, j))],
        out_specs=pl.BlockSpec((tm, tn), lambda j, i: (i, j)),
        out_shape=jax.ShapeDtypeStruct((m, n), F32),
        scratch_shapes=[pltpu.VMEM((k, tn), BF16)],
        compiler_params=_cparams(2),
        name=name,
    )(x, w, h)


def _ffn_up_kernel(x_ref, wg_ref, wu_ref, o_ref, wgb_ref, wub_ref):
    @pl.when(pl.program_id(1) == 0)
    def _():
        wgb_ref[...] = wg_ref[...].astype(BF16)
        wub_ref[...] = wu_ref[...].astype(BF16)

    x = x_ref[...]
    gate = jnp.dot(x, wgb_ref[...], preferred_element_type=F32)
    up = jnp.dot(x, wub_ref[...], preferred_element_type=F32)
    o_ref[...] = (jax.nn.silu(gate) * up).astype(o_ref.dtype)


def _ffn_up(x, wg, wu, layer, tm, tn):
    m, k = x.shape
    n = wg.shape[2]
    return pl.pallas_call(
        _ffn_up_kernel,
        grid=(n // tn, m // tm),
        in_specs=[pl.BlockSpec((tm, k), lambda j, i: (i, 0)),
                  pl.BlockSpec((None, k, tn), lambda j, i: (layer, 0, j)),
                  pl.BlockSpec((None, k, tn), lambda j, i: (layer, 0, j))],
        out_specs=pl.BlockSpec((tm, tn), lambda j, i: (i, j)),
        out_shape=jax.ShapeDtypeStruct((m, n), BF16),
        scratch_shapes=[pltpu.VMEM((k, tn), BF16), pltpu.VMEM((k, tn), BF16)],
        compiler_params=_cparams(2),
        name="ffn_up",
    )(x, wg, wu)


def _ple_kernel(n_ref, p_ref, h_ref, wpg_ref, wple_ref, o_ref, wpgb_ref, wpleb_ref):
    @pl.when(pl.program_id(1) == 0)
    def _():
        wpgb_ref[...] = wpg_ref[...].astype(BF16)
        wpleb_ref[...] = wple_ref[...].astype(BF16)

    gate = jax.nn.sigmoid(jnp.dot(n_ref[...], wpgb_ref[...], preferred_element_type=F32))
    emb = jnp.dot(p_ref[...].astype(BF16), wpleb_ref[...], preferred_element_type=F32)
    o_ref[...] = h_ref[...] + emb * gate


def _ple(n, p, h, wpg, wple, layer, tm, tn):
    m, k = n.shape
    kp = p.shape[2]
    nn = wpg.shape[2]
    return pl.pallas_call(
        _ple_kernel,
        grid=(nn // tn, m // tm),
        in_specs=[pl.BlockSpec((tm, k), lambda j, i: (i, 0)),
                  pl.BlockSpec((None, tm, kp), lambda j, i: (layer, i, 0)),
                  pl.BlockSpec((tm, tn), lambda j, i: (i, j)),
                  pl.BlockSpec((None, k, tn), lambda j, i: (layer, 0, j)),
                  pl.BlockSpec((None, kp, tn), lambda j, i: (layer, 0, j))],
        out_specs=pl.BlockSpec((tm, tn), lambda j, i: (i, j)),
        out_shape=jax.ShapeDtypeStruct((m, nn), F32),
        scratch_shapes=[pltpu.VMEM((k, tn), BF16), pltpu.VMEM((kp, tn), BF16)],
        compiler_params=_cparams(2),
        name="ple",
    )(n, p, h, wpg, wple)


def _softplus(x):
    return jnp.maximum(x, 0.0) + jnp.log1p(jnp.exp(-jnp.abs(x)))


def _expand_heads(v, e2_ref):
    lane = lax.broadcasted_iota(jnp.int32, v.shape, 1)
    v = jnp.where(lane < HEADS, v, 0.0)
    hi = v.astype(BF16)
    lo = (v - hi.astype(F32)).astype(BF16)
    return jnp.dot(jnp.concatenate([hi, lo], axis=1), e2_ref[...], preferred_element_type=F32)


def _mixer_kernel(*refs, mode):
    prompt = mode == "prompt"
    prep = mode == "sample_prep"
    it = iter(refs)
    z_ref, x_ref, u_ref, v_ref, bc_ref, dt_ref = (next(it) for _ in range(6))
    if not prompt:
        hx_refs = [next(it) for _ in range(3)]
        hbc_refs = [next(it) for _ in range(3)]
    if mode == "sample_final":
        yofft_ref = next(it)
        next(it)
    (cwx_ref, cwbc_ref, cbx_ref, cbbc_ref, dtb_ref, alog_ref, dsk_ref, gssd_ref, gcm_ref,
     e2_ref, ws_ref, bexp_ref) = (next(it) for _ in range(12))
    if prompt:
        y_ref, ssm_ref, tail_ref = next(it), next(it), next(it)
        st_ref, extx_ref, extbc_ref, xs_ref = (next(it) for _ in range(4))
    elif prep:
        xst_ref, bm_ref, cmt_ref, ea_ref = (next(it) for _ in range(4))
        extx_ref, extbc_ref, xs_ref = (next(it) for _ in range(3))
    else:
        y_ref, vn_ref = next(it), next(it)
        extx_ref, extbc_ref, xs_ref = (next(it) for _ in range(3))

    row = lax.broadcasted_iota(jnp.int32, (TILE, LANES), 0)
    col = lax.broadcasted_iota(jnp.int32, (TILE, LANES), 1)
    if prompt:
        pair_mask = col <= row
    else:
        pair_mask = (col <= row) & ((col >> 2) == (row >> 2))

    if prompt:
        @pl.when(pl.program_id(1) == 0)
        def _():
            st_ref[...] = jnp.zeros_like(st_ref)
            extx_ref[0:8, :] = jnp.zeros((8, D_SSD), F32)
            extbc_ref[0:8, :] = jnp.zeros((8, BC_W), F32)

    def conv(src_ref, ext_ref, w_ref, b_ref, hist_refs, width):
        if not prompt:
            ext_ref[0:8, :] = jnp.zeros((8, width), F32)
        ext_ref[8:8 + TILE, :] = src_ref[...]
        acc = b_ref[...] + w_ref[CONV_W - 1:CONV_W, :] * ext_ref[8:8 + TILE, :]
        rw = lax.broadcasted_iota(jnp.int32, (TILE, width), 0)
        for j in range(1, CONV_W):
            shifted = ext_ref[8 - j:8 - j + TILE, :]
            if not prompt:
                shifted = jnp.where((rw & (DEC_SEQ - 1)) >= j, shifted, hist_refs[j - 1][...])
            acc = acc + w_ref[CONV_W - 1 - j:CONV_W - j, :] * shifted
        if prompt:
            ext_ref[0:8, :] = ext_ref[TILE:TILE + 8, :]
        return jax.nn.silu(acc)

    xs_ref[...] = conv(x_ref, extx_ref, cwx_ref, cbx_ref, None if prompt else hx_refs, D_SSD)
    bcs = conv(bc_ref, extbc_ref, cwbc_ref, cbbc_ref, None if prompt else hbc_refs, BC_W)
    bm = [bcs[:, g * STATE:(g + 1) * STATE] for g in range(GROUPS)]
    cm = [bcs[:, (GROUPS + g) * STATE:(GROUPS + g + 1) * STATE] for g in range(GROUPS)]

    dt = _softplus(dt_ref[...] + dtb_ref[...])
    dta = dt * (-jnp.exp(alog_ref[...]))
    acum = dta
    if prompt:
        for s in (1, 2, 4, 8, 16, 32, 64):
            acum = acum + jnp.where(row >= s, pltpu.roll(acum, s, axis=0), 0.0)
        alast = acum[TILE - 1:TILE, :]
    else:
        pos = row & (DEC_SEQ - 1)
        for s in (1, 2):
            acum = acum + jnp.where(pos >= s, pltpu.roll(acum, s, axis=0), 0.0)
        alast = jnp.where(pos == DEC_SEQ - 1, acum, 0.0)
        alast = alast + jnp.where(pos == DEC_SEQ - 2, pltpu.roll(alast, TILE - 1, axis=0), 0.0)
        alast = alast + jnp.where(pos < DEC_SEQ - 2, pltpu.roll(alast, TILE - 2, axis=0), 0.0)
    ea = jnp.exp(acum)
    dsc = dt * jnp.exp(alast - acum)
    dsc_e = _expand_heads(dsc, e2_ref)

    if prep:
        xst_ref[...] = (xs_ref[...] * dsc_e).T.astype(BF16)
        bm_ref[...] = bcs[:, :GROUPS * STATE]
        for g in range(GROUPS):
            cmt_ref[g * STATE:(g + 1) * STATE, :] = cm[g].T
        ea_ref[...] = ea
        return

    dt_e = _expand_heads(dt, e2_ref)
    ea_e = _expand_heads(ea, e2_ref)
    acum_t = acum.T

    cb = [lax.dot_general(cm[g].astype(BF16), bm[g].astype(BF16), (((1,), (1,)), ((), ())),
                          preferred_element_type=F32) for g in range(GROUPS)]
    ydiag = []
    for j in range(HEADS // 2):
        w_pair = []
        for h in (2 * j, 2 * j + 1):
            seg = acum[:, h:h + 1] - acum_t[h:h + 1, :]
            dec = jnp.exp(jnp.where(pair_mask, seg, -jnp.inf))
            w_pair.append((cb[h // HEADS_PER_GROUP] * dec).astype(BF16))
        xdt = (xs_ref[:, j * LANES:(j + 1) * LANES] * dt_e[:, j * LANES:(j + 1) * LANES]).astype(BF16)
        lo_half = col < HEAD_DIM
        rhs = jnp.concatenate([jnp.where(lo_half, xdt, jnp.zeros_like(xdt)),
                               jnp.where(lo_half, jnp.zeros_like(xdt), xdt)], axis=0)
        ydiag.append(jnp.dot(jnp.concatenate(w_pair, axis=1), rhs, preferred_element_type=F32))

    if prompt:
        yoff = [jnp.dot(cm[g].astype(BF16), st_ref[:, g * GROUP_W:(g + 1) * GROUP_W].astype(BF16),
                        preferred_element_type=F32) for g in range(GROUPS)]
        for g in range(GROUPS):
            sl = slice(g * GROUP_W, (g + 1) * GROUP_W)
            xsc = (xs_ref[:, sl] * dsc_e[:, sl]).astype(BF16)
            upd = jnp.dot(bm[g].T.astype(BF16), xsc, preferred_element_type=F32)
            st_ref[:, sl] = st_ref[:, sl] * ea_e[TILE - 1:TILE, sl] + upd

        @pl.when(pl.program_id(1) == pl.num_programs(1) - 1)
        def _():
            ssm_ref[0] = st_ref[...].T
            tail_ref[0, :, :D_SSD] = extx_ref[0:8, :]
            tail_ref[0, :, D_SSD:] = extbc_ref[0:8, :]
    else:
        yoff_all = yofft_ref[...].T
        yoff = [yoff_all[:, g * GROUP_W:(g + 1) * GROUP_W] for g in range(GROUPS)]

    for g in range(GROUPS):
        sl = slice(g * GROUP_W, (g + 1) * GROUP_W)
        yd = jnp.concatenate(ydiag[g * 4:(g + 1) * 4], axis=1)
        y = yd + yoff[g] * ea_e[:, sl] + dsk_ref[:, sl] * xs_ref[:, sl]
        y = y * z_ref[:, sl]
        y = y * lax.rsqrt(jnp.mean(y * y, axis=-1, keepdims=True) + EPS)
        y_ref[:, sl] = (y * gssd_ref[:, sl]).astype(y_ref.dtype)

    gv = v_ref[...]
    vn = gv * lax.rsqrt(jnp.mean(gv * gv, axis=-1, keepdims=True) + EPS) * gcm_ref[...]
    if not prompt:
        vn_ref[...] = vn
    for g in range(CM_GROUPS):
        sl = slice(g * LANES, (g + 1) * LANES)
        wm = jnp.where(pair_mask, ws_ref[g], 0.0).astype(BF16)
        s = jnp.dot(wm, vn[:, sl].astype(BF16), preferred_element_type=F32) + bexp_ref[:, sl]
        y_ref[:, D_SSD + g * LANES:D_SSD + (g + 1) * LANES] = (u_ref[:, sl] * s).astype(y_ref.dtype)


STATE_SEQS = 4
SEQS_PER_TILE = TILE // DEC_SEQ


def _state_kernel(dec_ref, st_ref, xst_ref, bm_ref, cmt_ref, new_ref, yt_ref):
    b0 = pl.program_id(0) * STATE_SEQS
    bl0 = b0 % SEQS_PER_TILE

    @pl.when(bl0 == 0)
    def _():
        yt_ref[...] = jnp.zeros_like(yt_ref)

    row = lax.broadcasted_iota(jnp.int32, (TILE, LANES), 0)
    col = lax.broadcasted_iota(jnp.int32, (TILE, LANES), 1)
    for g in range(GROUPS):
        rs = slice(g * GROUP_W, (g + 1) * GROUP_W)
        cmt_g = cmt_ref[g * STATE:(g + 1) * STATE, :]
        bm_g = bm_ref[:, g * STATE:(g + 1) * STATE]
        states, ctms = [], []
        for q in range(STATE_SEQS):
            sg = st_ref[q, rs, :]
            states.append(sg.astype(BF16))
            ctms.append(jnp.where((col >> 2) == bl0 + q, cmt_g, 0.0).astype(BF16))
            bmk = jnp.where((row >> 2) == bl0 + q, bm_g, 0.0).astype(BF16)
            upd = jnp.dot(xst_ref[rs, :], bmk, preferred_element_type=F32)
            for hh in range(HEADS_PER_GROUP):
                hs = slice(hh * HEAD_DIM, (hh + 1) * HEAD_DIM)
                new_ref[q, g * GROUP_W + hh * HEAD_DIM:g * GROUP_W + (hh + 1) * HEAD_DIM, :] = (
                    sg[hs, :] * dec_ref[b0 + q, g * HEADS_PER_GROUP + hh] + upd[hs, :])
        yt_ref[rs, :] += jnp.dot(jnp.concatenate(states, axis=1), jnp.concatenate(ctms, axis=0),
                                 preferred_element_type=F32)


def _const_spec(shape):
    nd = len(shape)
    return pl.BlockSpec(shape, lambda *idx: (0,) * nd)


def _layer_spec(shape, layer):
    nd = len(shape)
    return pl.BlockSpec((None,) + tuple(shape), lambda *idx: (layer,) + (0,) * nd)


def _param_specs(layer):
    return [_layer_spec((CONV_W, D_SSD), layer), _layer_spec((CONV_W, BC_W), layer),
            _layer_spec((1, D_SSD), layer), _layer_spec((1, BC_W), layer),
            _layer_spec((1, LANES), layer), _layer_spec((1, LANES), layer),
            _layer_spec((1, D_SSD), layer), _layer_spec((1, D_SSD), layer), _layer_spec((1, D_CM), layer),
            _const_spec((2 * LANES, D_SSD)), _layer_spec((CM_GROUPS, TILE, TILE), layer),
            _layer_spec((TILE, D_CM), layer)]


def _mix_prompt(proj, dtraw, params, layer):
    tiles = SEQ // TILE

    def rt(b, c):
        return b * tiles + c

    in_specs = [pl.BlockSpec((TILE, 2048), lambda b, c: (rt(b, c), 0)),
                pl.BlockSpec((TILE, 2048), lambda b, c: (rt(b, c), 1)),
                pl.BlockSpec((TILE, 2048), lambda b, c: (rt(b, c), 2)),
                pl.BlockSpec((TILE, 2048), lambda b, c: (rt(b, c), 3)),
                pl.BlockSpec((TILE, BC_W), lambda b, c: (rt(b, c), 8)),
                pl.BlockSpec((TILE, LANES), lambda b, c: (rt(b, c), 0))]
    in_specs += _param_specs(layer)
    return pl.pallas_call(
        functools.partial(_mixer_kernel, mode="prompt"),
        grid=(BATCH, tiles),
        in_specs=in_specs,
        out_specs=[pl.BlockSpec((TILE, 2 * D_SSD), lambda b, c: (rt(b, c), 0)),
                   pl.BlockSpec((1, D_SSD, STATE), lambda b, c: (b, 0, 0)),
                   pl.BlockSpec((1, 8, D_SSD + BC_W), lambda b, c: (b, 0, 0))],
        out_shape=[jax.ShapeDtypeStruct((T_ALL, 2 * D_SSD), BF16),
                   jax.ShapeDtypeStruct((BATCH, D_SSD, STATE), F32),
                   jax.ShapeDtypeStruct((BATCH, 8, D_SSD + BC_W), F32)],
        scratch_shapes=[pltpu.VMEM((STATE, D_SSD), F32),
                        pltpu.VMEM((TILE + 8, D_SSD), F32),
                        pltpu.VMEM((TILE + 8, BC_W), F32),
                        pltpu.VMEM((TILE, D_SSD), F32)],
        compiler_params=_cparams(2),
        name="mix_prompt",
    )(proj, proj, proj, proj, proj, dtraw, *params)


def _sample_in_specs():
    base = T_PROMPT // TILE
    return [pl.BlockSpec((TILE, 2048), lambda t: (base + t, 0)),
            pl.BlockSpec((TILE, 2048), lambda t: (base + t, 1)),
            pl.BlockSpec((TILE, 2048), lambda t: (base + t, 2)),
            pl.BlockSpec((TILE, 2048), lambda t: (base + t, 3)),
            pl.BlockSpec((TILE, BC_W), lambda t: (base + t, 8)),
            pl.BlockSpec((TILE, LANES), lambda t: (base + t, 0))]


def _hist_specs(layer):
    return ([pl.BlockSpec((None, TILE, D_SSD), lambda t: (layer, t, 0))] * 3
            + [pl.BlockSpec((None, TILE, BC_W), lambda t: (layer, t, D_SSD // BC_W))] * 3)


def _sample_scratch():
    return [pltpu.VMEM((TILE + 8, D_SSD), F32), pltpu.VMEM((TILE + 8, BC_W), F32),
            pltpu.VMEM((TILE, D_SSD), F32)]


def _mix_sample_prep(proj, dtraw, hists, params, layer):
    nt = T_SAMPLE // TILE
    return pl.pallas_call(
        functools.partial(_mixer_kernel, mode="sample_prep"),
        grid=(nt,),
        in_specs=_sample_in_specs() + _hist_specs(layer) + _param_specs(layer),
        out_specs=[pl.BlockSpec((D_SSD, TILE), lambda t: (0, t)),
                   pl.BlockSpec((TILE, GROUPS * STATE), lambda t: (t, 0)),
                   pl.BlockSpec((GROUPS * STATE, TILE), lambda t: (0, t)),
                   pl.BlockSpec((TILE, LANES), lambda t: (t, 0))],
        out_shape=[jax.ShapeDtypeStruct((D_SSD, T_SAMPLE), BF16),
                   jax.ShapeDtypeStruct((T_SAMPLE, GROUPS * STATE), F32),
                   jax.ShapeDtypeStruct((GROUPS * STATE, T_SAMPLE), F32),
                   jax.ShapeDtypeStruct((T_SAMPLE, LANES), F32)],
        scratch_shapes=_sample_scratch(),
        compiler_params=_cparams(1),
        name="mix_sample_prep",
    )(proj, proj, proj, proj, proj, dtraw, *hists, *hists, *params)


def _mix_sample_final(proj, dtraw, hists, yofft, ybuf, params, layer):
    nt = T_SAMPLE // TILE
    base = T_PROMPT // TILE
    n_in = 6 + 6 + 1
    return pl.pallas_call(
        functools.partial(_mixer_kernel, mode="sample_final"),
        grid=(nt,),
        in_specs=(_sample_in_specs() + _hist_specs(layer)
                  + [pl.BlockSpec((D_SSD, TILE), lambda t: (0, t)), pl.BlockSpec(memory_space=pl.ANY)]
                  + _param_specs(layer)),
        out_specs=[pl.BlockSpec((TILE, 2 * D_SSD), lambda t: (base + t, 0)),
                   pl.BlockSpec((TILE, D_CM), lambda t: (t, 0))],
        out_shape=[jax.ShapeDtypeStruct((T_ALL, 2 * D_SSD), BF16),
                   jax.ShapeDtypeStruct((T_SAMPLE, D_CM), F32)],
        input_output_aliases={n_in: 0},
        scratch_shapes=_sample_scratch(),
        compiler_params=_cparams(1),
        name="mix_sample_final",
    )(proj, proj, proj, proj, proj, dtraw, *hists, *hists, yofft, ybuf, *params)


def _state_kernel_aliased(dec_ref, st_ref, xst_ref, bm_ref, cmt_ref, prev_ref, new_ref, yt_ref):
    del prev_ref
    _state_kernel(dec_ref, st_ref, xst_ref, bm_ref, cmt_ref, new_ref, yt_ref)


def _sample_state(dec, state, layer, xst, bm, cmt, new_all):
    per_tile = SEQS_PER_TILE // STATE_SEQS
    in_specs = [pl.BlockSpec(memory_space=pltpu.SMEM),
                pl.BlockSpec((None, STATE_SEQS, D_SSD, STATE), lambda b: (layer, b, 0, 0)),
                pl.BlockSpec((D_SSD, TILE), lambda b: (0, b // per_tile)),
                pl.BlockSpec((TILE, GROUPS * STATE), lambda b: (b // per_tile, 0)),
                pl.BlockSpec((GROUPS * STATE, TILE), lambda b: (0, b // per_tile))]
    args = [dec, state, xst, bm, cmt]
    aliases = {}
    body = _state_kernel
    if new_all is not None:
        in_specs.append(pl.BlockSpec(memory_space=pl.ANY))
        args.append(new_all)
        aliases = {5: 0}
        body = _state_kernel_aliased
    return pl.pallas_call(
        body,
        grid=(DEC_BATCH // STATE_SEQS,),
        in_specs=in_specs,
        out_specs=[pl.BlockSpec((None, STATE_SEQS, D_SSD, STATE), lambda b: (layer, b, 0, 0)),
                   pl.BlockSpec((D_SSD, TILE), lambda b: (0, b // per_tile))],
        out_shape=[jax.ShapeDtypeStruct((DEPTH, DEC_BATCH, D_SSD, STATE), F32),
                   jax.ShapeDtypeStruct((D_SSD, T_SAMPLE), F32)],
        input_output_aliases=aliases,
        compiler_params=_cparams(1),
        name="sample_state",
    )(*args)


def _expansion_matrix():
    e = np.zeros((2 * LANES, D_SSD), np.float32)
    for h in range(HEADS):
        e[h, h * HEAD_DIM:(h + 1) * HEAD_DIM] = 1.0
        e[LANES + h, h * HEAD_DIM:(h + 1) * HEAD_DIM] = 1.0
    return jnp.asarray(e, BF16)


def kernel(x_prompt, x_sample, state_ssm, state_conv, p_prompt, p_sample, g_mix, w_in, conv_w, conv_b, dt_bias, a_log, d_skip, g_ssd, g_cm, w_s, b_s, w_out, g_ffn, w_gate, w_up, w_down, g_pg, w_pg, w_ple, g_final):
    h = jnp.concatenate([x_prompt.reshape(T_PROMPT, D_MODEL), x_sample.reshape(T_SAMPLE, D_MODEL)], axis=0)
    p_all = jnp.concatenate([p_prompt.reshape(DEPTH, T_PROMPT, D_PLE), p_sample.reshape(DEPTH, T_SAMPLE, D_PLE)], axis=1)
    state_all = state_ssm.reshape(DEPTH, DEC_BATCH, D_SSD, STATE)
    w_in_t = jnp.transpose(w_in, (0, 2, 1))
    pad_heads = ((0, 0), (0, 0), (0, LANES - HEADS))
    reps = TILE // DEC_SEQ
    params_common = (conv_w[:, :, :D_SSD], conv_w[:, :, D_SSD:], conv_b[:, None, :D_SSD], conv_b[:, None, D_SSD:],
                     jnp.pad(dt_bias[:, None, :], pad_heads), jnp.pad(a_log[:, None, :], pad_heads),
                     jnp.repeat(d_skip, HEAD_DIM, axis=1)[:, None, :], g_ssd[:, None, :], g_cm[:, None, :],
                     _expansion_matrix())
    params_p = params_common + (w_s, jnp.repeat(jnp.swapaxes(b_s, 1, 2), LANES, axis=2))
    params_s = params_common + (
        jnp.tile(w_s[:, :, :DEC_SEQ, :DEC_SEQ], (1, 1, reps, reps)),
        jnp.tile(jnp.repeat(jnp.swapaxes(b_s[:, :, :DEC_SEQ], 1, 2), LANES, axis=2), (1, reps, 1)))
    hists = [jnp.pad(state_conv[:, :, CONV_W - 1 - j:, :], ((0, 0), (0, 0), (0, DEC_SEQ - j), (0, 0))
                     ).reshape(DEPTH, T_SAMPLE, D_SSD + BC_W) for j in range(1, CONV_W)]
    ssm_p, conv_p, conv_s, v_s = [], [], [], []
    ssm_s_all = None
    for i in range(DEPTH):
        n = _rms(h, g_mix[i], BF16)
        proj = _in_proj(n, w_in_t, i, 1088)
        dtraw = _in_proj_dt(n, w_in_t, i, 1088)

        ybuf, ssm_new_p, tail_p = _mix_prompt(proj, dtraw, params_p, i)
        xst, bmat, cmt, ea = _mix_sample_prep(proj, dtraw, hists, params_s, i)
        dec = ea[DEC_SEQ - 1::DEC_SEQ, :HEADS]
        ssm_s_all, yofft = _sample_state(dec, state_all, i, xst, bmat, cmt, ssm_s_all)
        y, vn_s = _mix_sample_final(proj, dtraw, hists, yofft, ybuf, params_s, i)

        h = _mm_res(y, w_out, i, h, 544, 1024, "out_proj", weight_buffers=1)
        f = _rms(h, g_ffn[i], BF16)
        act = _ffn_up(f, w_gate, w_up, i, 1088, 512)
        h = _mm_res(act, w_down, i, h, 544, 512, "ffn_down")
        npg = _rms(h, g_pg[i], BF16)
        h = _ple(npg, p_all, h, w_pg, w_ple, i, 544, 1024)

        xbc_s = jnp.concatenate([lax.slice(proj, (T_PROMPT, D_SSD), (T_ALL, 2 * D_SSD)),
                                 lax.slice(proj, (T_PROMPT, 4 * 2048), (T_ALL, 4 * 2048 + BC_W))], axis=1)
        ssm_p.append(ssm_new_p.reshape(BATCH, HEADS, HEAD_DIM, STATE))
        conv_p.append(tail_p[:, -(CONV_W - 1):])
        conv_s.append(xbc_s.reshape(DEC_BATCH, DEC_SEQ, -1)[:, -(CONV_W - 1):])
        v_s.append(vn_s.reshape(DEC_BATCH, DEC_SEQ, D_CM))
    out_p = _rms(h, g_final, F32, tm=T_SAMPLE, rows=T_PROMPT)
    out_s = _rms(h, g_final, F32, tm=T_SAMPLE, row0=T_PROMPT, rows=T_SAMPLE)
    return (out_p.reshape(BATCH, SEQ, D_MODEL), out_s.reshape(DEC_BATCH, DEC_SEQ, D_MODEL),
            jnp.stack(ssm_p), jnp.stack(conv_p),
            ssm_s_all.reshape(DEPTH, DEC_BATCH, HEADS, HEAD_DIM, STATE), jnp.stack(conv_s), jnp.stack(v_s))
```

```python
import functools

import jax
import jax.numpy as jnp
import numpy as np
from jax import lax
from jax.experimental import pallas as pl
from jax.experimental.pallas import tpu as pltpu

F32 = jnp.float32
BF16 = jnp.bfloat16

D_MODEL = 2048
BATCH = 4
SEQ = 2048
DEPTH = 4
DEC_BATCH = 128
DEC_SEQ = 4
D_SSD = 2048
HEAD_DIM = 64
HEADS = 32
GROUPS = 4
HEADS_PER_GROUP = HEADS // GROUPS
GROUP_W = D_SSD // GROUPS
STATE = 128
CONV_W = 4
BC_W = 2 * GROUPS * STATE
D_CM = 2048
CM_GROUPS = 16
D_FF = 5632
D_PLE = 256
EPS = 1e-6

T_PROMPT = BATCH * SEQ
T_SAMPLE = DEC_BATCH * DEC_SEQ
T_ALL = T_PROMPT + T_SAMPLE
TILE = 128
LANES = 128

VMEM_LIMIT = 56 * 1024 * 1024


def _cparams(n_axes):
    return pltpu.CompilerParams(dimension_semantics=("arbitrary",) * n_axes,
                                vmem_limit_bytes=VMEM_LIMIT)


def _rms_cast_kernel(x_ref, g_ref, o_ref):
    x = x_ref[...]
    y = x * lax.rsqrt(jnp.mean(x * x, axis=-1, keepdims=True) + EPS)
    o_ref[...] = (y * g_ref[...]).astype(o_ref.dtype)


def _rms(x, g, out_dtype, tm=544, row0=0, rows=None):
    d = x.shape[1]
    m = x.shape[0] if rows is None else rows
    blk0 = row0 // tm
    return pl.pallas_call(
        _rms_cast_kernel,
        grid=(m // tm,),
        in_specs=[pl.BlockSpec((tm, d), lambda i: (blk0 + i, 0)),
                  pl.BlockSpec((1, d), lambda i: (0, 0))],
        out_specs=pl.BlockSpec((tm, d), lambda i: (i, 0)),
        out_shape=jax.ShapeDtypeStruct((m, d), out_dtype),
        compiler_params=_cparams(1),
        name="rms_" + jnp.dtype(out_dtype).name,
    )(x, g.reshape(1, d))


IN_TN = 1024
IN_ALIGNED_TILES = (2 * D_SSD + BC_W) // IN_TN
IN_SHIFT = HEADS
IN_TILES = IN_ALIGNED_TILES + 2 * D_CM // IN_TN


_NT = (((1,), (1,)), ((), ()))


def _in_proj_kernel(x_ref, ss_ref, w_ref, wnext_ref, o_ref, wb_ref):
    j = pl.program_id(0)
    first = pl.program_id(1) == 0

    @pl.when(first & (j < IN_ALIGNED_TILES))
    def _():
        wb_ref[...] = w_ref[...].astype(BF16)

    @pl.when(first & (j >= IN_ALIGNED_TILES))
    def _():
        wb_ref[:IN_TN - IN_SHIFT, :] = w_ref[IN_SHIFT:, :].astype(BF16)
        wb_ref[IN_TN - IN_SHIFT:, :] = wnext_ref[...].astype(BF16)

    def project():
        return _row_scale(ss_ref) * lax.dot_general(x_ref[...], wb_ref[...], _NT, preferred_element_type=F32)

    @pl.when(j < D_SSD // IN_TN)
    def _():
        o_ref[...] = jax.nn.silu(project())

    @pl.when((j >= D_SSD // IN_TN) & (j < IN_ALIGNED_TILES))
    def _():
        o_ref[...] = project()

    @pl.when(j >= IN_ALIGNED_TILES)
    def _():
        o_ref[...] = jax.nn.gelu(project())


def _in_proj(x, ss, w_in_t, layer, tm):
    m, k = x.shape
    per_shift = IN_TN // IN_SHIFT

    def out_col(j):
        return jnp.where(j < IN_ALIGNED_TILES - 1, j, jnp.where(j == IN_ALIGNED_TILES - 1, IN_TILES - 1, j - 1))

    return pl.pallas_call(
        _in_proj_kernel,
        grid=(IN_TILES, m // tm),
        in_specs=[pl.BlockSpec((tm, k), lambda j, i: (i, 0)),
                  _ss_spec(ss.shape[0], tm),
                  pl.BlockSpec((None, IN_TN, k), lambda j, i: (layer, j, 0)),
                  pl.BlockSpec((None, IN_SHIFT, k),
                               lambda j, i: (layer, per_shift * (jnp.maximum(j, IN_ALIGNED_TILES - 1) + 1), 0))],
        out_specs=pl.BlockSpec((tm, IN_TN), lambda j, i: (i, out_col(j))),
        out_shape=jax.ShapeDtypeStruct((m, IN_TILES * IN_TN), F32),
        scratch_shapes=[pltpu.VMEM((IN_TN, k), BF16)],
        compiler_params=_cparams(2),
        name="in_proj",
    )(x, ss, w_in_t, w_in_t)


def _in_proj_dt_kernel(x_ref, ss_ref, w_ref, o_ref):
    o_ref[...] = _row_scale(ss_ref) * lax.dot_general(x_ref[...], w_ref[...].astype(BF16), _NT,
                                                      preferred_element_type=F32)


def _in_proj_dt(x, ss, w_in_t, layer, tm):
    m, k = x.shape
    blk = (2 * D_SSD + BC_W) // LANES
    parts = ss.shape[0]
    return pl.pallas_call(
        _in_proj_dt_kernel,
        grid=(m // tm,),
        in_specs=[pl.BlockSpec((tm, k), lambda i: (i, 0)),
                  pl.BlockSpec((parts, tm, LANES), lambda i: (0, i, 0)),
                  pl.BlockSpec((None, LANES, k), lambda i: (layer, blk, 0))],
        out_specs=pl.BlockSpec((tm, LANES), lambda i: (i, 0)),
        out_shape=jax.ShapeDtypeStruct((m, LANES), F32),
        compiler_params=_cparams(1),
        name="in_proj_dt",
    )(x, ss, w_in_t)


def _row_scale(ss_ref):
    tot = ss_ref[0]
    for part in range(1, ss_ref.shape[0]):
        tot = tot + ss_ref[part]
    return lax.rsqrt(tot[:, :1] * (1.0 / D_MODEL) + EPS)


def _emit_norm_inputs(h_new, g_ref, hg_ref, ss_ref):
    hg_ref[...] = (h_new * g_ref[...]).astype(hg_ref.dtype)
    ss_ref[...] = jnp.broadcast_to(jnp.sum(h_new * h_new, axis=1, keepdims=True), ss_ref.shape)


def _norm_outputs(m, n, tm, tn):
    specs = [pl.BlockSpec((tm, tn), lambda j, i: (i, j)), pl.BlockSpec((None, tm, LANES), lambda j, i: (j, i, 0))]
    shapes = [jax.ShapeDtypeStruct((m, n), BF16), jax.ShapeDtypeStruct((n // tn, m, LANES), F32)]
    return specs, shapes


def _ss_spec(parts, tm):
    return pl.BlockSpec((parts, tm, LANES), lambda j, i: (0, i, 0))


def _norm_stats_kernel(x_ref, g_ref, hg_ref, ss_ref):
    x = x_ref[...]
    _emit_norm_inputs(x, g_ref, hg_ref, ss_ref)


def _norm_stats(x, g, tm=544):
    m, d = x.shape
    return pl.pallas_call(
        _norm_stats_kernel,
        grid=(m // tm,),
        in_specs=[pl.BlockSpec((tm, d), lambda i: (i, 0)), pl.BlockSpec((1, d), lambda i: (0, 0))],
        out_specs=[pl.BlockSpec((tm, d), lambda i: (i, 0)), pl.BlockSpec((None, tm, LANES), lambda i: (0, i, 0))],
        out_shape=[jax.ShapeDtypeStruct((m, d), BF16), jax.ShapeDtypeStruct((1, m, LANES), F32)],
        compiler_params=_cparams(1),
        name="norm_stats",
    )(x, g.reshape(1, d))


def _mm_res_kernel(x_ref, w_ref, h_ref, g_ref, o_ref, hg_ref, ss_ref, wb_ref):
    @pl.when(pl.program_id(1) == 0)
    def _():
        wb_ref[...] = w_ref[...].astype(BF16)

    h_new = h_ref[...] + jnp.dot(x_ref[...], wb_ref[...], preferred_element_type=F32)
    o_ref[...] = h_new
    _emit_norm_inputs(h_new, g_ref, hg_ref, ss_ref)


def _mm_res(x, w, layer, h, g_next, tm, tn, name, weight_buffers=2):
    m, k = x.shape
    n = w.shape[2]
    nspecs, nshapes = _norm_outputs(m, n, tm, tn)
    return pl.pallas_call(
        _mm_res_kernel,
        grid=(n // tn, m // tm),
        in_specs=[pl.BlockSpec((tm, k), lambda j, i: (i, 0)),
                  pl.BlockSpec((None, k, tn), lambda j, i: (layer, 0, j),
                               pipeline_mode=pl.Buffered(weight_buffers)),
                  pl.BlockSpec((tm, tn), lambda j, i: (i, j)),
                  pl.BlockSpec((None, 1, tn), lambda j, i: (layer, 0, j))],
        out_specs=[pl.BlockSpec((tm, tn), lambda j, i: (i, j))] + nspecs,
        out_shape=[jax.ShapeDtypeStruct((m, n), F32)] + nshapes,
        scratch_shapes=[pltpu.VMEM((k, tn), BF16)],
        compiler_params=_cparams(2),
        name=name,
    )(x, w, h, g_next)


def _ffn_up_kernel(x_ref, ss_ref, wg_ref, wu_ref, o_ref, wgb_ref, wub_ref):
    @pl.when(pl.program_id(1) == 0)
    def _():
        wgb_ref[...] = wg_ref[...].astype(BF16)
        wub_ref[...] = wu_ref[...].astype(BF16)

    x = x_ref[...]
    r = _row_scale(ss_ref)
    gate = r * jnp.dot(x, wgb_ref[...], preferred_element_type=F32)
    up = r * jnp.dot(x, wub_ref[...], preferred_element_type=F32)
    o_ref[...] = (jax.nn.silu(gate) * up).astype(o_ref.dtype)


def _ffn_up(x, ss, wg, wu, layer, tm, tn):
    m, k = x.shape
    n = wg.shape[2]
    return pl.pallas_call(
        _ffn_up_kernel,
        grid=(n // tn, m // tm),
        in_specs=[pl.BlockSpec((tm, k), lambda j, i: (i, 0)),
                  _ss_spec(ss.shape[0], tm),
                  pl.BlockSpec((None, k, tn), lambda j, i: (layer, 0, j)),
                  pl.BlockSpec((None, k, tn), lambda j, i: (layer, 0, j))],
        out_specs=pl.BlockSpec((tm, tn), lambda j, i: (i, j)),
        out_shape=jax.ShapeDtypeStruct((m, n), BF16),
        scratch_shapes=[pltpu.VMEM((k, tn), BF16), pltpu.VMEM((k, tn), BF16)],
        compiler_params=_cparams(2),
        name="ffn_up",
    )(x, ss, wg, wu)


def _ple_kernel(*refs, emit_norm):
    n_ref, ss_ref, p_ref, h_ref, wpg_ref, wple_ref = refs[:6]
    if emit_norm:
        g_ref, o_ref, hg_ref, ssn_ref, wpgb_ref, wpleb_ref = refs[6:]
    else:
        o_ref, wpgb_ref, wpleb_ref = refs[6:]

    @pl.when(pl.program_id(1) == 0)
    def _():
        wpgb_ref[...] = wpg_ref[...].astype(BF16)
        wpleb_ref[...] = wple_ref[...].astype(BF16)

    gate = jax.nn.sigmoid(_row_scale(ss_ref) * jnp.dot(n_ref[...], wpgb_ref[...], preferred_element_type=F32))
    emb = jnp.dot(p_ref[...].astype(BF16), wpleb_ref[...], preferred_element_type=F32)
    h_new = h_ref[...] + emb * gate
    o_ref[...] = h_new
    if emit_norm:
        _emit_norm_inputs(h_new, g_ref, hg_ref, ssn_ref)


def _ple(n, ss, p, h, wpg, wple, g_next, layer, tm, tn):
    m, k = n.shape
    kp = p.shape[2]
    nn = wpg.shape[2]
    emit_norm = g_next is not None
    in_specs = [pl.BlockSpec((tm, k), lambda j, i: (i, 0)),
                _ss_spec(ss.shape[0], tm),
                pl.BlockSpec((None, tm, kp), lambda j, i: (layer, i, 0)),
                pl.BlockSpec((tm, tn), lambda j, i: (i, j)),
                pl.BlockSpec((None, k, tn), lambda j, i: (layer, 0, j)),
                pl.BlockSpec((None, kp, tn), lambda j, i: (layer, 0, j))]
    args = [n, ss, p, h, wpg, wple]
    out_specs = [pl.BlockSpec((tm, tn), lambda j, i: (i, j))]
    out_shape = [jax.ShapeDtypeStruct((m, nn), F32)]
    if emit_norm:
        in_specs.append(pl.BlockSpec((1, tn), lambda j, i: (0, j)))
        args.append(g_next)
        nspecs, nshapes = _norm_outputs(m, nn, tm, tn)
        out_specs += nspecs
        out_shape += nshapes
    return pl.pallas_call(
        functools.partial(_ple_kernel, emit_norm=emit_norm),
        grid=(nn // tn, m // tm),
        in_specs=in_specs,
        out_specs=out_specs,
        out_shape=out_shape,
        scratch_shapes=[pltpu.VMEM((k, tn), BF16), pltpu.VMEM((kp, tn), BF16)],
        compiler_params=_cparams(2),
        name="ple",
    )(*args)


def _softplus(x):
    return jnp.maximum(x, 0.0) + jnp.log1p(jnp.exp(-jnp.abs(x)))


def _expand_heads(v, e2_ref):
    lane = lax.broadcasted_iota(jnp.int32, v.shape, 1)
    v = jnp.where(lane < HEADS, v, 0.0)
    hi = v.astype(BF16)
    lo = (v - hi.astype(F32)).astype(BF16)
    return jnp.dot(jnp.concatenate([hi, lo], axis=1), e2_ref[...], preferred_element_type=F32)


def _mixer_kernel(*refs, mode):
    prompt = mode == "prompt"
    prep = mode == "sample_prep"
    it = iter(refs)
    z_ref, x_ref, u_ref, v_ref, bc_ref, dt_ref = (next(it) for _ in range(6))
    if not prompt:
        sc_ref = next(it)
    if mode == "sample_final":
        yofft_ref = next(it)
        next(it)
    (cwx_ref, cwbc_ref, cbx_ref, cbbc_ref, dtb_ref, alog_ref, dsk_ref, gssd_ref, gcm_ref,
     e2_ref, ws_ref, bexp_ref) = (next(it) for _ in range(12))
    if prompt:
        y_ref, ssm_ref, tail_ref = next(it), next(it), next(it)
        st_ref, extx_ref, extbc_ref, xs_ref = (next(it) for _ in range(4))
    elif prep:
        xst_ref, bm_ref, cmt_ref, ea_ref = (next(it) for _ in range(4))
        extx_ref, extbc_ref, xs_ref, hist_ref = (next(it) for _ in range(4))
    else:
        y_ref, vn_ref = next(it), next(it)
        extx_ref, extbc_ref, xs_ref, hist_ref = (next(it) for _ in range(4))

    row = lax.broadcasted_iota(jnp.int32, (TILE, LANES), 0)
    col = lax.broadcasted_iota(jnp.int32, (TILE, LANES), 1)
    if prompt:
        pair_mask = col <= row
    else:
        pair_mask = (col <= row) & ((col >> 2) == (row >> 2))

    if prompt:
        @pl.when(pl.program_id(1) == 0)
        def _():
            st_ref[...] = jnp.zeros_like(st_ref)
            extx_ref[0:8, :] = jnp.zeros((8, D_SSD), F32)
            extbc_ref[0:8, :] = jnp.zeros((8, BC_W), F32)

    else:
        hist_ref[...] = jnp.zeros_like(hist_ref)
        for j in range(1, CONV_W):
            for l in range(j):
                for ct in range((D_SSD + BC_W) // LANES):
                    hist_ref[j - 1, ct, pl.ds(l, TILE // DEC_SEQ, stride=DEC_SEQ), :] = (
                        sc_ref[CONV_W - 1 - j + l, :, ct * LANES:(ct + 1) * LANES])

    def conv(src_ref, ext_ref, w_ref, b_ref, col0, width):
        if not prompt:
            ext_ref[0:8, :] = jnp.zeros((8, width), F32)
        ext_ref[8:8 + TILE, :] = src_ref[...]
        acc = b_ref[...] + w_ref[CONV_W - 1:CONV_W, :] * ext_ref[8:8 + TILE, :]
        rw = lax.broadcasted_iota(jnp.int32, (TILE, width), 0)
        for j in range(1, CONV_W):
            shifted = ext_ref[8 - j:8 - j + TILE, :]
            if not prompt:
                hist = jnp.concatenate([hist_ref[j - 1, ct] for ct in range(col0 // LANES, (col0 + width) // LANES)],
                                       axis=1)
                shifted = jnp.where((rw & (DEC_SEQ - 1)) >= j, shifted, hist)
            acc = acc + w_ref[CONV_W - 1 - j:CONV_W - j, :] * shifted
        if prompt:
            ext_ref[0:8, :] = ext_ref[TILE:TILE + 8, :]
        return jax.nn.silu(acc)

    xs_ref[...] = conv(x_ref, extx_ref, cwx_ref, cbx_ref, 0, D_SSD)
    bcs = conv(bc_ref, extbc_ref, cwbc_ref, cbbc_ref, D_SSD, BC_W)
    bm = [bcs[:, g * STATE:(g + 1) * STATE] for g in range(GROUPS)]
    cm = [bcs[:, (GROUPS + g) * STATE:(GROUPS + g + 1) * STATE] for g in range(GROUPS)]

    dt = _softplus(dt_ref[...] + dtb_ref[...])
    dta = dt * (-jnp.exp(alog_ref[...]))
    acum = dta
    if prompt:
        for s in (1, 2, 4, 8, 16, 32, 64):
            acum = acum + jnp.where(row >= s, pltpu.roll(acum, s, axis=0), 0.0)
        alast = acum[TILE - 1:TILE, :]
    else:
        pos = row & (DEC_SEQ - 1)
        for s in (1, 2):
            acum = acum + jnp.where(pos >= s, pltpu.roll(acum, s, axis=0), 0.0)
        alast = jnp.where(pos == DEC_SEQ - 1, acum, 0.0)
        alast = alast + jnp.where(pos == DEC_SEQ - 2, pltpu.roll(alast, TILE - 1, axis=0), 0.0)
        alast = alast + jnp.where(pos < DEC_SEQ - 2, pltpu.roll(alast, TILE - 2, axis=0), 0.0)
    ea = jnp.exp(acum)
    dsc = dt * jnp.exp(alast - acum)
    dsc_e = _expand_heads(dsc, e2_ref)

    if prep:
        xst_ref[...] = (xs_ref[...] * dsc_e).T.astype(BF16)
        bm_ref[...] = bcs[:, :GROUPS * STATE]
        for g in range(GROUPS):
            cmt_ref[g * STATE:(g + 1) * STATE, :] = cm[g].T
        ea_ref[...] = ea
        return

    dt_e = _expand_heads(dt, e2_ref)
    ea_e = _expand_heads(ea, e2_ref)
    acum_t = acum.T

    cb = [lax.dot_general(cm[g].astype(BF16), bm[g].astype(BF16), (((1,), (1,)), ((), ())),
                          preferred_element_type=F32) for g in range(GROUPS)]
    ydiag = []
    for j in range(HEADS // 2):
        w_pair = []
        for h in (2 * j, 2 * j + 1):
            seg = acum[:, h:h + 1] - acum_t[h:h + 1, :]
            dec = jnp.exp(jnp.where(pair_mask, seg, -jnp.inf))
            w_pair.append((cb[h // HEADS_PER_GROUP] * dec).astype(BF16))
        xdt = (xs_ref[:, j * LANES:(j + 1) * LANES] * dt_e[:, j * LANES:(j + 1) * LANES]).astype(BF16)
        lo_half = col < HEAD_DIM
        rhs = jnp.concatenate([jnp.where(lo_half, xdt, jnp.zeros_like(xdt)),
                               jnp.where(lo_half, jnp.zeros_like(xdt), xdt)], axis=0)
        ydiag.append(jnp.dot(jnp.concatenate(w_pair, axis=1), rhs, preferred_element_type=F32))

    if prompt:
        yoff = [jnp.dot(cm[g].astype(BF16), st_ref[:, g * GROUP_W:(g + 1) * GROUP_W].astype(BF16),
                        preferred_element_type=F32) for g in range(GROUPS)]
        for g in range(GROUPS):
            sl = slice(g * GROUP_W, (g + 1) * GROUP_W)
            xsc = (xs_ref[:, sl] * dsc_e[:, sl]).astype(BF16)
            upd = jnp.dot(bm[g].T.astype(BF16), xsc, preferred_element_type=F32)
            st_ref[:, sl] = st_ref[:, sl] * ea_e[TILE - 1:TILE, sl] + upd

        @pl.when(pl.program_id(1) == pl.num_programs(1) - 1)
        def _():
            ssm_ref[0] = st_ref[...].T
            tail_ref[0, :, :D_SSD] = extx_ref[0:8, :]
            tail_ref[0, :, D_SSD:] = extbc_ref[0:8, :]
    else:
        yoff_all = yofft_ref[...].T
        yoff = [yoff_all[:, g * GROUP_W:(g + 1) * GROUP_W] for g in range(GROUPS)]

    for g in range(GROUPS):
        sl = slice(g * GROUP_W, (g + 1) * GROUP_W)
        yd = jnp.concatenate(ydiag[g * 4:(g + 1) * 4], axis=1)
        y = yd + yoff[g] * ea_e[:, sl] + dsk_ref[:, sl] * xs_ref[:, sl]
        y = y * z_ref[:, sl]
        y = y * lax.rsqrt(jnp.mean(y * y, axis=-1, keepdims=True) + EPS)
        y_ref[:, sl] = (y * gssd_ref[:, sl]).astype(y_ref.dtype)

    gv = v_ref[...]
    vn = gv * lax.rsqrt(jnp.mean(gv * gv, axis=-1, keepdims=True) + EPS) * gcm_ref[...]
    if not prompt:
        vn_ref[...] = vn
    for g in range(CM_GROUPS):
        sl = slice(g * LANES, (g + 1) * LANES)
        wm = jnp.where(pair_mask, ws_ref[g], 0.0).astype(BF16)
        s = jnp.dot(wm, vn[:, sl].astype(BF16), preferred_element_type=F32) + bexp_ref[:, sl]
        y_ref[:, D_SSD + g * LANES:D_SSD + (g + 1) * LANES] = (u_ref[:, sl] * s).astype(y_ref.dtype)


STATE_SEQS = 4
SEQS_PER_TILE = TILE // DEC_SEQ


def _state_kernel(dec_ref, st_ref, xst_ref, bm_ref, cmt_ref, new_ref, yt_ref):
    b0 = pl.program_id(0) * STATE_SEQS
    bl0 = b0 % SEQS_PER_TILE

    @pl.when(bl0 == 0)
    def _():
        yt_ref[...] = jnp.zeros_like(yt_ref)

    row = lax.broadcasted_iota(jnp.int32, (TILE, LANES), 0)
    col = lax.broadcasted_iota(jnp.int32, (TILE, LANES), 1)
    for g in range(GROUPS):
        rs = slice(g * GROUP_W, (g + 1) * GROUP_W)
        cmt_g = cmt_ref[g * STATE:(g + 1) * STATE, :]
        bm_g = bm_ref[:, g * STATE:(g + 1) * STATE]
        states, ctms = [], []
        for q in range(STATE_SEQS):
            sg = st_ref[q, rs, :]
            states.append(sg.astype(BF16))
            ctms.append(jnp.where((col >> 2) == bl0 + q, cmt_g, 0.0).astype(BF16))
            bmk = jnp.where((row >> 2) == bl0 + q, bm_g, 0.0).astype(BF16)
            upd = jnp.dot(xst_ref[rs, :], bmk, preferred_element_type=F32)
            for hh in range(HEADS_PER_GROUP):
                hs = slice(hh * HEAD_DIM, (hh + 1) * HEAD_DIM)
                new_ref[q, g * GROUP_W + hh * HEAD_DIM:g * GROUP_W + (hh + 1) * HEAD_DIM, :] = (
                    sg[hs, :] * dec_ref[b0 + q, g * HEADS_PER_GROUP + hh] + upd[hs, :])
        yt_ref[rs, :] += jnp.dot(jnp.concatenate(states, axis=1), jnp.concatenate(ctms, axis=0),
                                 preferred_element_type=F32)


def _const_spec(shape):
    nd = len(shape)
    return pl.BlockSpec(shape, lambda *idx: (0,) * nd)


def _layer_spec(shape, layer):
    nd = len(shape)
    return pl.BlockSpec((None,) + tuple(shape), lambda *idx: (layer,) + (0,) * nd)


def _param_specs(layer):
    return [_layer_spec((CONV_W, D_SSD), layer), _layer_spec((CONV_W, BC_W), layer),
            _layer_spec((1, D_SSD), layer), _layer_spec((1, BC_W), layer),
            _layer_spec((1, LANES), layer), _layer_spec((1, LANES), layer),
            _layer_spec((1, D_SSD), layer), _layer_spec((1, D_SSD), layer), _layer_spec((1, D_CM), layer),
            _const_spec((2 * LANES, D_SSD)), _layer_spec((CM_GROUPS, TILE, TILE), layer),
            _layer_spec((TILE, D_CM), layer)]


def _mix_prompt(proj, dtraw, params, layer):
    tiles = SEQ // TILE

    def rt(b, c):
        return b * tiles + c

    in_specs = [pl.BlockSpec((TILE, 2048), lambda b, c: (rt(b, c), 0)),
                pl.BlockSpec((TILE, 2048), lambda b, c: (rt(b, c), 1)),
                pl.BlockSpec((TILE, 2048), lambda b, c: (rt(b, c), 2)),
                pl.BlockSpec((TILE, 2048), lambda b, c: (rt(b, c), 3)),
                pl.BlockSpec((TILE, BC_W), lambda b, c: (rt(b, c), 8)),
                pl.BlockSpec((TILE, LANES), lambda b, c: (rt(b, c), 0))]
    in_specs += _param_specs(layer)
    return pl.pallas_call(
        functools.partial(_mixer_kernel, mode="prompt"),
        grid=(BATCH, tiles),
        in_specs=in_specs,
        out_specs=[pl.BlockSpec((TILE, 2 * D_SSD), lambda b, c: (rt(b, c), 0)),
                   pl.BlockSpec((1, D_SSD, STATE), lambda b, c: (b, 0, 0)),
                   pl.BlockSpec((1, 8, D_SSD + BC_W), lambda b, c: (b, 0, 0))],
        out_shape=[jax.ShapeDtypeStruct((T_ALL, 2 * D_SSD), BF16),
                   jax.ShapeDtypeStruct((BATCH, D_SSD, STATE), F32),
                   jax.ShapeDtypeStruct((BATCH, 8, D_SSD + BC_W), F32)],
        scratch_shapes=[pltpu.VMEM((STATE, D_SSD), F32),
                        pltpu.VMEM((TILE + 8, D_SSD), F32),
                        pltpu.VMEM((TILE + 8, BC_W), F32),
                        pltpu.VMEM((TILE, D_SSD), F32)],
        compiler_params=_cparams(2),
        name="mix_prompt",
    )(proj, proj, proj, proj, proj, dtraw, *params)


def _sample_in_specs():
    base = T_PROMPT // TILE
    return [pl.BlockSpec((TILE, 2048), lambda t: (base + t, 0)),
            pl.BlockSpec((TILE, 2048), lambda t: (base + t, 1)),
            pl.BlockSpec((TILE, 2048), lambda t: (base + t, 2)),
            pl.BlockSpec((TILE, 2048), lambda t: (base + t, 3)),
            pl.BlockSpec((TILE, BC_W), lambda t: (base + t, 8)),
            pl.BlockSpec((TILE, LANES), lambda t: (base + t, 0))]


def _hist_specs(layer):
    return [pl.BlockSpec((None, CONV_W - 1, TILE // DEC_SEQ, D_SSD + BC_W), lambda t: (layer, 0, t, 0))]


def _sample_scratch():
    return [pltpu.VMEM((TILE + 8, D_SSD), F32), pltpu.VMEM((TILE + 8, BC_W), F32),
            pltpu.VMEM((TILE, D_SSD), F32),
            pltpu.VMEM((CONV_W - 1, (D_SSD + BC_W) // LANES, TILE, LANES), F32)]


def _mix_sample_prep(proj, dtraw, hists, params, layer):
    nt = T_SAMPLE // TILE
    return pl.pallas_call(
        functools.partial(_mixer_kernel, mode="sample_prep"),
        grid=(nt,),
        in_specs=_sample_in_specs() + _hist_specs(layer) + _param_specs(layer),
        out_specs=[pl.BlockSpec((D_SSD, TILE), lambda t: (0, t)),
                   pl.BlockSpec((TILE, GROUPS * STATE), lambda t: (t, 0)),
                   pl.BlockSpec((GROUPS * STATE, TILE), lambda t: (0, t)),
                   pl.BlockSpec((TILE, LANES), lambda t: (t, 0))],
        out_shape=[jax.ShapeDtypeStruct((D_SSD, T_SAMPLE), BF16),
                   jax.ShapeDtypeStruct((T_SAMPLE, GROUPS * STATE), F32),
                   jax.ShapeDtypeStruct((GROUPS * STATE, T_SAMPLE), F32),
                   jax.ShapeDtypeStruct((T_SAMPLE, LANES), F32)],
        scratch_shapes=_sample_scratch(),
        compiler_params=_cparams(1),
        name="mix_sample_prep",
    )(proj, proj, proj, proj, proj, dtraw, hists, *params)


def _mix_sample_final(proj, dtraw, hists, yofft, ybuf, params, layer):
    nt = T_SAMPLE // TILE
    base = T_PROMPT // TILE
    n_in = 6 + 1 + 1
    return pl.pallas_call(
        functools.partial(_mixer_kernel, mode="sample_final"),
        grid=(nt,),
        in_specs=(_sample_in_specs() + _hist_specs(layer)
                  + [pl.BlockSpec((D_SSD, TILE), lambda t: (0, t)), pl.BlockSpec(memory_space=pl.ANY)]
                  + _param_specs(layer)),
        out_specs=[pl.BlockSpec((TILE, 2 * D_SSD), lambda t: (base + t, 0)),
                   pl.BlockSpec((TILE, D_CM), lambda t: (t, 0))],
        out_shape=[jax.ShapeDtypeStruct((T_ALL, 2 * D_SSD), BF16),
                   jax.ShapeDtypeStruct((T_SAMPLE, D_CM), F32)],
        input_output_aliases={n_in: 0},
        scratch_shapes=_sample_scratch(),
        compiler_params=_cparams(1),
        name="mix_sample_final",
    )(proj, proj, proj, proj, proj, dtraw, hists, yofft, ybuf, *params)


def _state_kernel_aliased(dec_ref, st_ref, xst_ref, bm_ref, cmt_ref, prev_ref, new_ref, yt_ref):
    del prev_ref
    _state_kernel(dec_ref, st_ref, xst_ref, bm_ref, cmt_ref, new_ref, yt_ref)


def _sample_state(dec, state, layer, xst, bm, cmt, new_all):
    per_tile = SEQS_PER_TILE // STATE_SEQS
    in_specs = [pl.BlockSpec(memory_space=pltpu.SMEM),
                pl.BlockSpec((None, STATE_SEQS, D_SSD, STATE), lambda b: (layer, b, 0, 0)),
                pl.BlockSpec((D_SSD, TILE), lambda b: (0, b // per_tile)),
                pl.BlockSpec((TILE, GROUPS * STATE), lambda b: (b // per_tile, 0)),
                pl.BlockSpec((GROUPS * STATE, TILE), lambda b: (0, b // per_tile))]
    args = [dec, state, xst, bm, cmt]
    aliases = {}
    body = _state_kernel
    if new_all is not None:
        in_specs.append(pl.BlockSpec(memory_space=pl.ANY))
        args.append(new_all)
        aliases = {5: 0}
        body = _state_kernel_aliased
    return pl.pallas_call(
        body,
        grid=(DEC_BATCH // STATE_SEQS,),
        in_specs=in_specs,
        out_specs=[pl.BlockSpec((None, STATE_SEQS, D_SSD, STATE), lambda b: (layer, b, 0, 0)),
                   pl.BlockSpec((D_SSD, TILE), lambda b: (0, b // per_tile))],
        out_shape=[jax.ShapeDtypeStruct((DEPTH, DEC_BATCH, D_SSD, STATE), F32),
                   jax.ShapeDtypeStruct((D_SSD, T_SAMPLE), F32)],
        input_output_aliases=aliases,
        compiler_params=_cparams(1),
        name="sample_state",
    )(*args)


def _expansion_matrix():
    e = np.zeros((2 * LANES, D_SSD), np.float32)
    for h in range(HEADS):
        e[h, h * HEAD_DIM:(h + 1) * HEAD_DIM] = 1.0
        e[LANES + h, h * HEAD_DIM:(h + 1) * HEAD_DIM] = 1.0
    return jnp.asarray(e, BF16)


def kernel(x_prompt, x_sample, state_ssm, state_conv, p_prompt, p_sample, g_mix, w_in, conv_w, conv_b, dt_bias, a_log, d_skip, g_ssd, g_cm, w_s, b_s, w_out, g_ffn, w_gate, w_up, w_down, g_pg, w_pg, w_ple, g_final):
    h = jnp.concatenate([x_prompt.reshape(T_PROMPT, D_MODEL), x_sample.reshape(T_SAMPLE, D_MODEL)], axis=0)
    p_all = jnp.concatenate([p_prompt.reshape(DEPTH, T_PROMPT, D_PLE), p_sample.reshape(DEPTH, T_SAMPLE, D_PLE)], axis=1)
    state_all = state_ssm.reshape(DEPTH, DEC_BATCH, D_SSD, STATE)
    w_in_t = jnp.transpose(w_in, (0, 2, 1))
    pad_heads = ((0, 0), (0, 0), (0, LANES - HEADS))
    reps = TILE // DEC_SEQ
    params_common = (conv_w[:, :, :D_SSD], conv_w[:, :, D_SSD:], conv_b[:, None, :D_SSD], conv_b[:, None, D_SSD:],
                     jnp.pad(dt_bias[:, None, :], pad_heads), jnp.pad(a_log[:, None, :], pad_heads),
                     jnp.repeat(d_skip, HEAD_DIM, axis=1)[:, None, :], g_ssd[:, None, :], g_cm[:, None, :],
                     _expansion_matrix())
    params_p = params_common + (w_s, jnp.repeat(jnp.swapaxes(b_s, 1, 2), LANES, axis=2))
    params_s = params_common + (
        jnp.tile(w_s[:, :, :DEC_SEQ, :DEC_SEQ], (1, 1, reps, reps)),
        jnp.tile(jnp.repeat(jnp.swapaxes(b_s[:, :, :DEC_SEQ], 1, 2), LANES, axis=2), (1, reps, 1)))
    hists = jnp.transpose(state_conv, (0, 2, 1, 3))
    ssm_p, conv_p, conv_s, v_s = [], [], [], []
    ssm_s_all = None
    g_ffn3, g_pg3 = g_ffn[:, None, :], g_pg[:, None, :]
    hg, ss = _norm_stats(h, g_mix[0])
    for i in range(DEPTH):
        proj = _in_proj(hg, ss, w_in_t, i, 1088)
        dtraw = _in_proj_dt(hg, ss, w_in_t, i, 1088)

        ybuf, ssm_new_p, tail_p = _mix_prompt(proj, dtraw, params_p, i)
        xst, bmat, cmt, ea = _mix_sample_prep(proj, dtraw, hists, params_s, i)
        dec = ea[DEC_SEQ - 1::DEC_SEQ, :HEADS]
        ssm_s_all, yofft = _sample_state(dec, state_all, i, xst, bmat, cmt, ssm_s_all)
        y, vn_s = _mix_sample_final(proj, dtraw, hists, yofft, ybuf, params_s, i)

        h, hg, ss = _mm_res(y, w_out, i, h, g_ffn3, 544, 1024, "out_proj", weight_buffers=1)
        act = _ffn_up(hg, ss, w_gate, w_up, i, 1088, 512)
        h, hg, ss = _mm_res(act, w_down, i, h, g_pg3, 544, 512, "ffn_down")
        if i + 1 < DEPTH:
            h, hg, ss = _ple(hg, ss, p_all, h, w_pg, w_ple, g_mix[i + 1].reshape(1, -1), i, 544, 1024)
        else:
            h, = _ple(hg, ss, p_all, h, w_pg, w_ple, None, i, 544, 1024)

        xbc_s = jnp.concatenate([lax.slice(proj, (T_PROMPT, D_SSD), (T_ALL, 2 * D_SSD)),
                                 lax.slice(proj, (T_PROMPT, 4 * 2048), (T_ALL, 4 * 2048 + BC_W))], axis=1)
        ssm_p.append(ssm_new_p.reshape(BATCH, HEADS, HEAD_DIM, STATE))
        conv_p.append(tail_p[:, -(CONV_W - 1):])
        conv_s.append(xbc_s.reshape(DEC_BATCH, DEC_SEQ, -1)[:, -(CONV_W - 1):])
        v_s.append(vn_s.reshape(DEC_BATCH, DEC_SEQ, D_CM))
    out_p = _rms(h, g_final, F32, tm=T_SAMPLE, rows=T_PROMPT)
    out_s = _rms(h, g_final, F32, tm=T_SAMPLE, row0=T_PROMPT, rows=T_SAMPLE)
    return (out_p.reshape(BATCH, SEQ, D_MODEL), out_s.reshape(DEC_BATCH, DEC_SEQ, D_MODEL),
            jnp.stack(ssm_p), jnp.stack(conv_p),
            ssm_s_all.reshape(DEPTH, DEC_BATCH, HEADS, HEAD_DIM, STATE), jnp.stack(conv_s), jnp.stack(v_s))
```

```python
import functools

import jax
import jax.numpy as jnp
import numpy as np
from jax import lax
from jax.experimental import pallas as pl
from jax.experimental.pallas import tpu as pltpu

F32 = jnp.float32
BF16 = jnp.bfloat16

D_MODEL = 2048
BATCH = 4
SEQ = 2048
DEPTH = 4
DEC_BATCH = 128
DEC_SEQ = 4
D_SSD = 2048
HEAD_DIM = 64
HEADS = 32
GROUPS = 4
HEADS_PER_GROUP = HEADS // GROUPS
GROUP_W = D_SSD // GROUPS
STATE = 128
CONV_W = 4
BC_W = 2 * GROUPS * STATE
D_CM = 2048
CM_GROUPS = 16
D_FF = 5632
D_PLE = 256
EPS = 1e-6

T_PROMPT = BATCH * SEQ
T_SAMPLE = DEC_BATCH * DEC_SEQ
T_ALL = T_PROMPT + T_SAMPLE
TILE = 128
PROMPT_TILES = SEQ // TILE
TAIL_ROWS = TILE + T_SAMPLE
OUT_CHUNK = 256
LANES = 128

VMEM_LIMIT = 56 * 1024 * 1024


def _cparams(n_axes):
    return pltpu.CompilerParams(dimension_semantics=("arbitrary",) * n_axes,
                                vmem_limit_bytes=VMEM_LIMIT)


def _rms_cast_kernel(x_ref, g_ref, o_ref):
    x = x_ref[...]
    y = x * lax.rsqrt(jnp.mean(x * x, axis=-1, keepdims=True) + EPS)
    o_ref[...] = (y * g_ref[...]).astype(o_ref.dtype)


def _rms(x, g, out_dtype, tm=544, row0=0, rows=None):
    d = x.shape[1]
    m = x.shape[0] if rows is None else rows
    blk0 = row0 // tm
    return pl.pallas_call(
        _rms_cast_kernel,
        grid=(m // tm,),
        in_specs=[pl.BlockSpec((tm, d), lambda i: (blk0 + i, 0)),
                  pl.BlockSpec((1, d), lambda i: (0, 0))],
        out_specs=pl.BlockSpec((tm, d), lambda i: (i, 0)),
        out_shape=jax.ShapeDtypeStruct((m, d), out_dtype),
        compiler_params=_cparams(1),
        name="rms_" + jnp.dtype(out_dtype).name,
    )(x, g.reshape(1, d))


IN_TN = 1024
IN_ALIGNED_TILES = (2 * D_SSD + BC_W) // IN_TN
IN_SHIFT = HEADS
IN_TILES = IN_ALIGNED_TILES + 2 * D_CM // IN_TN


_NT = (((1,), (1,)), ((), ()))


def _in_proj_kernel(x_ref, ss_ref, w_ref, wnext_ref, o_ref, wb_ref):
    j = pl.program_id(0)
    first = pl.program_id(1) == 0

    @pl.when(first & (j < IN_ALIGNED_TILES))
    def _():
        wb_ref[...] = w_ref[...].astype(BF16)

    @pl.when(first & (j >= IN_ALIGNED_TILES))
    def _():
        wb_ref[:IN_TN - IN_SHIFT, :] = w_ref[IN_SHIFT:, :].astype(BF16)
        wb_ref[IN_TN - IN_SHIFT:, :] = wnext_ref[...].astype(BF16)

    def project():
        return _row_scale(ss_ref) * lax.dot_general(x_ref[...], wb_ref[...], _NT, preferred_element_type=F32)

    @pl.when(j < D_SSD // IN_TN)
    def _():
        o_ref[...] = jax.nn.silu(project())

    @pl.when((j >= D_SSD // IN_TN) & (j < IN_ALIGNED_TILES))
    def _():
        o_ref[...] = project()

    @pl.when(j >= IN_ALIGNED_TILES)
    def _():
        o_ref[...] = jax.nn.gelu(project())


def _in_proj(x, ss, w_in_t, layer, tm):
    m, k = x.shape
    per_shift = IN_TN // IN_SHIFT

    def out_col(j):
        return jnp.where(j < IN_ALIGNED_TILES - 1, j, jnp.where(j == IN_ALIGNED_TILES - 1, IN_TILES - 1, j - 1))

    return pl.pallas_call(
        _in_proj_kernel,
        grid=(IN_TILES, m // tm),
        in_specs=[pl.BlockSpec((tm, k), lambda j, i: (i, 0)),
                  _ss_spec(ss.shape[0], tm),
                  pl.BlockSpec((None, IN_TN, k), lambda j, i: (layer, j, 0)),
                  pl.BlockSpec((None, IN_SHIFT, k),
                               lambda j, i: (layer, per_shift * (jnp.maximum(j, IN_ALIGNED_TILES - 1) + 1), 0))],
        out_specs=pl.BlockSpec((tm, IN_TN), lambda j, i: (i, out_col(j))),
        out_shape=jax.ShapeDtypeStruct((m, IN_TILES * IN_TN), F32),
        scratch_shapes=[pltpu.VMEM((IN_TN, k), BF16)],
        compiler_params=_cparams(2),
        name="in_proj",
    )(x, ss, w_in_t, w_in_t)


def _in_proj_dt_kernel(x_ref, ss_ref, w_ref, o_ref):
    o_ref[...] = _row_scale(ss_ref) * lax.dot_general(x_ref[...], w_ref[...].astype(BF16), _NT,
                                                      preferred_element_type=F32)


def _in_proj_dt(x, ss, w_in_t, layer, tm):
    m, k = x.shape
    blk = (2 * D_SSD + BC_W) // LANES
    parts = ss.shape[0]
    return pl.pallas_call(
        _in_proj_dt_kernel,
        grid=(m // tm,),
        in_specs=[pl.BlockSpec((tm, k), lambda i: (i, 0)),
                  pl.BlockSpec((parts, tm, LANES), lambda i: (0, i, 0)),
                  pl.BlockSpec((None, LANES, k), lambda i: (layer, blk, 0))],
        out_specs=pl.BlockSpec((tm, LANES), lambda i: (i, 0)),
        out_shape=jax.ShapeDtypeStruct((m, LANES), F32),
        compiler_params=_cparams(1),
        name="in_proj_dt",
    )(x, ss, w_in_t)


def _row_scale(ss_ref):
    tot = ss_ref[0]
    for part in range(1, ss_ref.shape[0]):
        tot = tot + ss_ref[part]
    return lax.rsqrt(tot[:, :1] * (1.0 / D_MODEL) + EPS)


def _emit_norm_inputs(h_new, g_ref, hg_ref, ss_ref):
    hg_ref[...] = (h_new * g_ref[...]).astype(hg_ref.dtype)
    ss_ref[...] = jnp.broadcast_to(jnp.sum(h_new * h_new, axis=1, keepdims=True), ss_ref.shape)


def _norm_outputs(m, n, tm, tn):
    specs = [pl.BlockSpec((tm, tn), lambda j, i: (i, j)), pl.BlockSpec((None, tm, LANES), lambda j, i: (j, i, 0))]
    shapes = [jax.ShapeDtypeStruct((m, n), BF16), jax.ShapeDtypeStruct((n // tn, m, LANES), F32)]
    return specs, shapes


def _ss_spec(parts, tm):
    return pl.BlockSpec((parts, tm, LANES), lambda j, i: (0, i, 0))


def _norm_stats_kernel(x_ref, g_ref, hg_ref, ss_ref):
    x = x_ref[...]
    _emit_norm_inputs(x, g_ref, hg_ref, ss_ref)


def _norm_stats(x, g, tm=544):
    m, d = x.shape
    return pl.pallas_call(
        _norm_stats_kernel,
        grid=(m // tm,),
        in_specs=[pl.BlockSpec((tm, d), lambda i: (i, 0)), pl.BlockSpec((1, d), lambda i: (0, 0))],
        out_specs=[pl.BlockSpec((tm, d), lambda i: (i, 0)), pl.BlockSpec((None, tm, LANES), lambda i: (0, i, 0))],
        out_shape=[jax.ShapeDtypeStruct((m, d), BF16), jax.ShapeDtypeStruct((1, m, LANES), F32)],
        compiler_params=_cparams(1),
        name="norm_stats",
    )(x, g.reshape(1, d))


def _mm_res_kernel(x_ref, w_ref, h_ref, g_ref, o_ref, hg_ref, ss_ref, wb_ref):
    @pl.when(pl.program_id(1) == 0)
    def _():
        wb_ref[...] = w_ref[...].astype(BF16)

    h_new = h_ref[...] + jnp.dot(x_ref[...], wb_ref[...], preferred_element_type=F32)
    o_ref[...] = h_new
    _emit_norm_inputs(h_new, g_ref, hg_ref, ss_ref)


def _mm_res(x, w, layer, h, g_next, tm, tn, name, weight_buffers=2):
    m, k = x.shape
    n = w.shape[2]
    nspecs, nshapes = _norm_outputs(m, n, tm, tn)
    return pl.pallas_call(
        _mm_res_kernel,
        grid=(n // tn, m // tm),
        in_specs=[pl.BlockSpec((tm, k), lambda j, i: (i, 0)),
                  pl.BlockSpec((None, k, tn), lambda j, i: (layer, 0, j),
                               pipeline_mode=pl.Buffered(weight_buffers)),
                  pl.BlockSpec((tm, tn), lambda j, i: (i, j)),
                  pl.BlockSpec((None, 1, tn), lambda j, i: (layer, 0, j))],
        out_specs=[pl.BlockSpec((tm, tn), lambda j, i: (i, j))] + nspecs,
        out_shape=[jax.ShapeDtypeStruct((m, n), F32)] + nshapes,
        scratch_shapes=[pltpu.VMEM((k, tn), BF16)],
        compiler_params=_cparams(2),
        name=name,
    )(x, w, h, g_next)


def _ffn_up_kernel(x_ref, ss_ref, wg_ref, wu_ref, o_ref, wgb_ref, wub_ref):
    @pl.when(pl.program_id(1) == 0)
    def _():
        wgb_ref[...] = wg_ref[...].astype(BF16)
        wub_ref[...] = wu_ref[...].astype(BF16)

    x = x_ref[...]
    r = _row_scale(ss_ref)
    gate = r * jnp.dot(x, wgb_ref[...], preferred_element_type=F32)
    up = r * jnp.dot(x, wub_ref[...], preferred_element_type=F32)
    o_ref[...] = (jax.nn.silu(gate) * up).astype(o_ref.dtype)


def _ffn_up(x, ss, wg, wu, layer, tm, tn):
    m, k = x.shape
    n = wg.shape[2]
    return pl.pallas_call(
        _ffn_up_kernel,
        grid=(n // tn, m // tm),
        in_specs=[pl.BlockSpec((tm, k), lambda j, i: (i, 0)),
                  _ss_spec(ss.shape[0], tm),
                  pl.BlockSpec((None, k, tn), lambda j, i: (layer, 0, j)),
                  pl.BlockSpec((None, k, tn), lambda j, i: (layer, 0, j))],
        out_specs=pl.BlockSpec((tm, tn), lambda j, i: (i, j)),
        out_shape=jax.ShapeDtypeStruct((m, n), BF16),
        scratch_shapes=[pltpu.VMEM((k, tn), BF16), pltpu.VMEM((k, tn), BF16)],
        compiler_params=_cparams(2),
        name="ffn_up",
    )(x, ss, wg, wu)


def _ple_kernel(*refs, emit_norm):
    n_ref, ss_ref, p_ref, h_ref, wpg_ref, wple_ref = refs[:6]
    if emit_norm:
        g_ref, o_ref, hg_ref, ssn_ref, wpgb_ref, wpleb_ref = refs[6:]
    else:
        o_ref, wpgb_ref, wpleb_ref = refs[6:]

    @pl.when(pl.program_id(1) == 0)
    def _():
        wpgb_ref[...] = wpg_ref[...].astype(BF16)
        wpleb_ref[...] = wple_ref[...].astype(BF16)

    gate = jax.nn.sigmoid(_row_scale(ss_ref) * jnp.dot(n_ref[...], wpgb_ref[...], preferred_element_type=F32))
    emb = jnp.dot(p_ref[...].astype(BF16), wpleb_ref[...], preferred_element_type=F32)
    h_new = h_ref[...] + emb * gate
    o_ref[...] = h_new
    if emit_norm:
        _emit_norm_inputs(h_new, g_ref, hg_ref, ssn_ref)


def _ple(n, ss, p, h, wpg, wple, g_next, layer, tm, tn):
    m, k = n.shape
    kp = p.shape[2]
    nn = wpg.shape[2]
    emit_norm = g_next is not None
    in_specs = [pl.BlockSpec((tm, k), lambda j, i: (i, 0)),
                _ss_spec(ss.shape[0], tm),
                pl.BlockSpec((None, tm, kp), lambda j, i: (layer, i, 0)),
                pl.BlockSpec((tm, tn), lambda j, i: (i, j)),
                pl.BlockSpec((None, k, tn), lambda j, i: (layer, 0, j)),
                pl.BlockSpec((None, kp, tn), lambda j, i: (layer, 0, j))]
    args = [n, ss, p, h, wpg, wple]
    out_specs = [pl.BlockSpec((tm, tn), lambda j, i: (i, j))]
    out_shape = [jax.ShapeDtypeStruct((m, nn), F32)]
    if emit_norm:
        in_specs.append(pl.BlockSpec((1, tn), lambda j, i: (0, j)))
        args.append(g_next)
        nspecs, nshapes = _norm_outputs(m, nn, tm, tn)
        out_specs += nspecs
        out_shape += nshapes
    return pl.pallas_call(
        functools.partial(_ple_kernel, emit_norm=emit_norm),
        grid=(nn // tn, m // tm),
        in_specs=in_specs,
        out_specs=out_specs,
        out_shape=out_shape,
        scratch_shapes=[pltpu.VMEM((k, tn), BF16), pltpu.VMEM((kp, tn), BF16)],
        compiler_params=_cparams(2),
        name="ple",
    )(*args)


def _softplus(x):
    return jnp.maximum(x, 0.0) + jnp.log1p(jnp.exp(-jnp.abs(x)))


def _expand_heads(v, e2_ref):
    lane = lax.broadcasted_iota(jnp.int32, v.shape, 1)
    v = jnp.where(lane < HEADS, v, 0.0)
    hi = v.astype(BF16)
    lo = (v - hi.astype(F32)).astype(BF16)
    return jnp.dot(jnp.concatenate([hi, lo], axis=1), e2_ref[...], preferred_element_type=F32)


def _mixer_kernel(*refs, mode):
    prompt = mode == "prompt"
    prep = mode == "sample_prep"
    it = iter(refs)
    z_ref, x_ref, u_ref, v_ref, bc_ref, dt_ref = (next(it) for _ in range(6))
    if not prompt:
        sc_ref = next(it)
    if mode == "sample_final":
        yofft_ref = next(it)
        next(it)
    (cwx_ref, cwbc_ref, cbx_ref, cbbc_ref, dtb_ref, alog_ref, dsk_ref, gssd_ref, gcm_ref,
     e2_ref, ws_ref, bexp_ref) = (next(it) for _ in range(12))
    if prompt:
        wout_ref, h_ref, gn_ref = next(it), next(it), next(it)
        hn_ref, hg_ref, ss_ref, ssm_ref, tail_ref, ylast_ref = (next(it) for _ in range(6))
        st_ref, extx_ref, extbc_ref, xs_ref, y_ref = (next(it) for _ in range(5))
        chunk = pl.program_id(0) & (PROMPT_TILES - 1)
    elif prep:
        xst_ref, bm_ref, cmt_ref, ea_ref = (next(it) for _ in range(4))
        extx_ref, extbc_ref, xs_ref, hist_ref = (next(it) for _ in range(4))
    else:
        y_ref, vn_ref = next(it), next(it)
        extx_ref, extbc_ref, xs_ref, hist_ref = (next(it) for _ in range(4))

    row = lax.broadcasted_iota(jnp.int32, (TILE, LANES), 0)
    col = lax.broadcasted_iota(jnp.int32, (TILE, LANES), 1)
    if prompt:
        pair_mask = col <= row
    else:
        pair_mask = (col <= row) & ((col >> 2) == (row >> 2))

    if prompt:
        @pl.when(pl.program_id(0) == 0)
        def _():
            y_ref[...] = jnp.zeros_like(y_ref)

        @pl.when(chunk == 0)
        def _():
            st_ref[...] = jnp.zeros_like(st_ref)
            extx_ref[0:8, :] = jnp.zeros((8, D_SSD), F32)
            extbc_ref[0:8, :] = jnp.zeros((8, BC_W), F32)

        sumsq = jnp.zeros((TILE, 1), F32)
        for n0 in range(0, D_MODEL, OUT_CHUNK):
            cs = slice(n0, n0 + OUT_CHUNK)
            h_new = h_ref[:, cs] + jnp.dot(y_ref[...], wout_ref[:, cs], preferred_element_type=F32)
            hn_ref[:, cs] = h_new
            hg_ref[:, cs] = (h_new * gn_ref[:, cs]).astype(hg_ref.dtype)
            sumsq = sumsq + jnp.sum(h_new * h_new, axis=1, keepdims=True)
        ss_ref[...] = jnp.broadcast_to(sumsq, ss_ref.shape)
    else:
        hist_ref[...] = jnp.zeros_like(hist_ref)
        for j in range(1, CONV_W):
            for l in range(j):
                for ct in range((D_SSD + BC_W) // LANES):
                    hist_ref[j - 1, ct, pl.ds(l, TILE // DEC_SEQ, stride=DEC_SEQ), :] = (
                        sc_ref[CONV_W - 1 - j + l, :, ct * LANES:(ct + 1) * LANES])

    def conv(src_ref, ext_ref, w_ref, b_ref, col0, width):
        if not prompt:
            ext_ref[0:8, :] = jnp.zeros((8, width), F32)
        ext_ref[8:8 + TILE, :] = src_ref[...]
        acc = b_ref[...] + w_ref[CONV_W - 1:CONV_W, :] * ext_ref[8:8 + TILE, :]
        rw = lax.broadcasted_iota(jnp.int32, (TILE, width), 0)
        for j in range(1, CONV_W):
            shifted = ext_ref[8 - j:8 - j + TILE, :]
            if not prompt:
                hist = jnp.concatenate([hist_ref[j - 1, ct] for ct in range(col0 // LANES, (col0 + width) // LANES)],
                                       axis=1)
                shifted = jnp.where((rw & (DEC_SEQ - 1)) >= j, shifted, hist)
            acc = acc + w_ref[CONV_W - 1 - j:CONV_W - j, :] * shifted
        if prompt:
            ext_ref[0:8, :] = ext_ref[TILE:TILE + 8, :]
        return jax.nn.silu(acc)

    xs_ref[...] = conv(x_ref, extx_ref, cwx_ref, cbx_ref, 0, D_SSD)
    bcs = conv(bc_ref, extbc_ref, cwbc_ref, cbbc_ref, D_SSD, BC_W)
    bm = [bcs[:, g * STATE:(g + 1) * STATE] for g in range(GROUPS)]
    cm = [bcs[:, (GROUPS + g) * STATE:(GROUPS + g + 1) * STATE] for g in range(GROUPS)]

    dt = _softplus(dt_ref[...] + dtb_ref[...])
    dta = dt * (-jnp.exp(alog_ref[...]))
    acum = dta
    if prompt:
        for s in (1, 2, 4, 8, 16, 32, 64):
            acum = acum + jnp.where(row >= s, pltpu.roll(acum, s, axis=0), 0.0)
        alast = acum[TILE - 1:TILE, :]
    else:
        pos = row & (DEC_SEQ - 1)
        for s in (1, 2):
            acum = acum + jnp.where(pos >= s, pltpu.roll(acum, s, axis=0), 0.0)
        alast = jnp.where(pos == DEC_SEQ - 1, acum, 0.0)
        alast = alast + jnp.where(pos == DEC_SEQ - 2, pltpu.roll(alast, TILE - 1, axis=0), 0.0)
        alast = alast + jnp.where(pos < DEC_SEQ - 2, pltpu.roll(alast, TILE - 2, axis=0), 0.0)
    ea = jnp.exp(acum)
    dsc = dt * jnp.exp(alast - acum)
    dsc_e = _expand_heads(dsc, e2_ref)

    if prep:
        xst_ref[...] = (xs_ref[...] * dsc_e).T.astype(BF16)
        bm_ref[...] = bcs[:, :GROUPS * STATE]
        for g in range(GROUPS):
            cmt_ref[g * STATE:(g + 1) * STATE, :] = cm[g].T
        ea_ref[...] = ea
        return

    dt_e = _expand_heads(dt, e2_ref)
    ea_e = _expand_heads(ea, e2_ref)
    acum_t = acum.T

    cb = [lax.dot_general(cm[g].astype(BF16), bm[g].astype(BF16), (((1,), (1,)), ((), ())),
                          preferred_element_type=F32) for g in range(GROUPS)]
    ydiag = []
    for j in range(HEADS // 2):
        w_pair = []
        for h in (2 * j, 2 * j + 1):
            seg = acum[:, h:h + 1] - acum_t[h:h + 1, :]
            dec = jnp.exp(jnp.where(pair_mask, seg, -jnp.inf))
            w_pair.append((cb[h // HEADS_PER_GROUP] * dec).astype(BF16))
        xdt = (xs_ref[:, j * LANES:(j + 1) * LANES] * dt_e[:, j * LANES:(j + 1) * LANES]).astype(BF16)
        lo_half = col < HEAD_DIM
        rhs = jnp.concatenate([jnp.where(lo_half, xdt, jnp.zeros_like(xdt)),
                               jnp.where(lo_half, jnp.zeros_like(xdt), xdt)], axis=0)
        ydiag.append(jnp.dot(jnp.concatenate(w_pair, axis=1), rhs, preferred_element_type=F32))

    if prompt:
        yoff = [jnp.dot(cm[g].astype(BF16), st_ref[:, g * GROUP_W:(g + 1) * GROUP_W].astype(BF16),
                        preferred_element_type=F32) for g in range(GROUPS)]
        for g in range(GROUPS):
            sl = slice(g * GROUP_W, (g + 1) * GROUP_W)
            xsc = (xs_ref[:, sl] * dsc_e[:, sl]).astype(BF16)
            upd = jnp.dot(bm[g].T.astype(BF16), xsc, preferred_element_type=F32)
            st_ref[:, sl] = st_ref[:, sl] * ea_e[TILE - 1:TILE, sl] + upd
    else:
        yoff_all = yofft_ref[...].T
        yoff = [yoff_all[:, g * GROUP_W:(g + 1) * GROUP_W] for g in range(GROUPS)]

    for g in range(GROUPS):
        sl = slice(g * GROUP_W, (g + 1) * GROUP_W)
        yd = jnp.concatenate(ydiag[g * 4:(g + 1) * 4], axis=1)
        y = yd + yoff[g] * ea_e[:, sl] + dsk_ref[:, sl] * xs_ref[:, sl]
        y = y * z_ref[:, sl]
        y = y * lax.rsqrt(jnp.mean(y * y, axis=-1, keepdims=True) + EPS)
        y_ref[:, sl] = (y * gssd_ref[:, sl]).astype(y_ref.dtype)

    gv = v_ref[...]
    vn = gv * lax.rsqrt(jnp.mean(gv * gv, axis=-1, keepdims=True) + EPS) * gcm_ref[...]
    if not prompt:
        vn_ref[...] = vn
    for g in range(CM_GROUPS):
        sl = slice(g * LANES, (g + 1) * LANES)
        wm = jnp.where(pair_mask, ws_ref[g], 0.0).astype(BF16)
        s = jnp.dot(wm, vn[:, sl].astype(BF16), preferred_element_type=F32) + bexp_ref[:, sl]
        y_ref[:, D_SSD + g * LANES:D_SSD + (g + 1) * LANES] = (u_ref[:, sl] * s).astype(y_ref.dtype)

    if prompt:
        @pl.when(chunk == PROMPT_TILES - 1)
        def _():
            ssm_ref[0] = st_ref[...].T
            tail_ref[0, :, :D_SSD] = extx_ref[0:8, :]
            tail_ref[0, :, D_SSD:] = extbc_ref[0:8, :]

        @pl.when(pl.program_id(0) == pl.num_programs(0) - 1)
        def _():
            ylast_ref[...] = y_ref[...]


STATE_SEQS = 4
SEQS_PER_TILE = TILE // DEC_SEQ


def _state_kernel(dec_ref, st_ref, xst_ref, bm_ref, cmt_ref, new_ref, yt_ref):
    b0 = pl.program_id(0) * STATE_SEQS
    bl0 = b0 % SEQS_PER_TILE

    @pl.when(bl0 == 0)
    def _():
        yt_ref[...] = jnp.zeros_like(yt_ref)

    row = lax.broadcasted_iota(jnp.int32, (TILE, LANES), 0)
    col = lax.broadcasted_iota(jnp.int32, (TILE, LANES), 1)
    for g in range(GROUPS):
        rs = slice(g * GROUP_W, (g + 1) * GROUP_W)
        cmt_g = cmt_ref[g * STATE:(g + 1) * STATE, :]
        bm_g = bm_ref[:, g * STATE:(g + 1) * STATE]
        states, ctms = [], []
        for q in range(STATE_SEQS):
            sg = st_ref[q, rs, :]
            states.append(sg.astype(BF16))
            ctms.append(jnp.where((col >> 2) == bl0 + q, cmt_g, 0.0).astype(BF16))
            bmk = jnp.where((row >> 2) == bl0 + q, bm_g, 0.0).astype(BF16)
            upd = jnp.dot(xst_ref[rs, :], bmk, preferred_element_type=F32)
            for hh in range(HEADS_PER_GROUP):
                hs = slice(hh * HEAD_DIM, (hh + 1) * HEAD_DIM)
                new_ref[q, g * GROUP_W + hh * HEAD_DIM:g * GROUP_W + (hh + 1) * HEAD_DIM, :] = (
                    sg[hs, :] * dec_ref[b0 + q, g * HEADS_PER_GROUP + hh] + upd[hs, :])
        yt_ref[rs, :] += jnp.dot(jnp.concatenate(states, axis=1), jnp.concatenate(ctms, axis=0),
                                 preferred_element_type=F32)


def _const_spec(shape):
    nd = len(shape)
    return pl.BlockSpec(shape, lambda *idx: (0,) * nd)


def _layer_spec(shape, layer):
    nd = len(shape)
    return pl.BlockSpec((None,) + tuple(shape), lambda *idx: (layer,) + (0,) * nd)


def _param_specs(layer):
    return [_layer_spec((CONV_W, D_SSD), layer), _layer_spec((CONV_W, BC_W), layer),
            _layer_spec((1, D_SSD), layer), _layer_spec((1, BC_W), layer),
            _layer_spec((1, LANES), layer), _layer_spec((1, LANES), layer),
            _layer_spec((1, D_SSD), layer), _layer_spec((1, D_SSD), layer), _layer_spec((1, D_CM), layer),
            _const_spec((2 * LANES, D_SSD)), _layer_spec((CM_GROUPS, TILE, TILE), layer),
            _layer_spec((TILE, D_CM), layer)]


def _cast_kernel(w_ref, o_ref):
    o_ref[...] = w_ref[...].astype(o_ref.dtype)


def _cast_bf16(w, layer, tr=512):
    _, k, n = w.shape
    return pl.pallas_call(
        _cast_kernel,
        grid=(k // tr,),
        in_specs=[pl.BlockSpec((None, tr, n), lambda i: (layer, i, 0))],
        out_specs=pl.BlockSpec((tr, n), lambda i: (i, 0)),
        out_shape=jax.ShapeDtypeStruct((k, n), BF16),
        compiler_params=_cparams(1),
        name="cast_bf16",
    )(w)


def _mix_prompt(proj, dtraw, params, w_out_bf, h, g_next, layer):
    tiles = T_PROMPT // TILE

    def prev(t):
        return jnp.maximum(t - 1, 0)

    in_specs = [pl.BlockSpec((TILE, 2048), lambda t: (t, 0)),
                pl.BlockSpec((TILE, 2048), lambda t: (t, 1)),
                pl.BlockSpec((TILE, 2048), lambda t: (t, 2)),
                pl.BlockSpec((TILE, 2048), lambda t: (t, 3)),
                pl.BlockSpec((TILE, BC_W), lambda t: (t, 8)),
                pl.BlockSpec((TILE, LANES), lambda t: (t, 0))]
    in_specs += _param_specs(layer)
    in_specs += [pl.BlockSpec((2 * D_SSD, D_MODEL), lambda t: (0, 0), pipeline_mode=pl.Buffered(1)),
                 pl.BlockSpec((TILE, D_MODEL), lambda t: (prev(t), 0)),
                 pl.BlockSpec((None, 1, D_MODEL), lambda t: (layer, 0, 0))]
    return pl.pallas_call(
        functools.partial(_mixer_kernel, mode="prompt"),
        grid=(tiles,),
        in_specs=in_specs,
        out_specs=[pl.BlockSpec((TILE, D_MODEL), lambda t: (prev(t), 0)),
                   pl.BlockSpec((TILE, D_MODEL), lambda t: (prev(t), 0)),
                   pl.BlockSpec((None, TILE, LANES), lambda t: (0, prev(t), 0)),
                   pl.BlockSpec((1, D_SSD, STATE), lambda t: (t // PROMPT_TILES, 0, 0)),
                   pl.BlockSpec((1, 8, D_SSD + BC_W), lambda t: (t // PROMPT_TILES, 0, 0)),
                   pl.BlockSpec((TILE, 2 * D_SSD), lambda t: (0, 0))],
        out_shape=[jax.ShapeDtypeStruct((T_ALL, D_MODEL), F32),
                   jax.ShapeDtypeStruct((T_ALL, D_MODEL), BF16),
                   jax.ShapeDtypeStruct((1, T_ALL, LANES), F32),
                   jax.ShapeDtypeStruct((BATCH, D_SSD, STATE), F32),
                   jax.ShapeDtypeStruct((BATCH, 8, D_SSD + BC_W), F32),
                   jax.ShapeDtypeStruct((TAIL_ROWS, 2 * D_SSD), BF16)],
        scratch_shapes=[pltpu.VMEM((STATE, D_SSD), F32),
                        pltpu.VMEM((TILE + 8, D_SSD), F32),
                        pltpu.VMEM((TILE + 8, BC_W), F32),
                        pltpu.VMEM((TILE, D_SSD), F32),
                        pltpu.VMEM((TILE, 2 * D_SSD), BF16)],
        compiler_params=_cparams(1),
        name="mix_prompt",
    )(proj, proj, proj, proj, proj, dtraw, *params, w_out_bf, h, g_next)


def _out_proj_tail_kernel(y_ref, w_ref, h_ref, g_ref, hn_in, hg_in, ss_in, hn_ref, hg_ref, ss_ref):
    del hn_in, hg_in, ss_in
    h_new = h_ref[...] + jnp.dot(y_ref[...], w_ref[...], preferred_element_type=F32)
    hn_ref[...] = h_new
    _emit_norm_inputs(h_new, g_ref, hg_ref, ss_ref)


def _out_proj_tail(ytail, w_out_bf, h, g_next, layer, h_new, hg, ss):
    blk0 = (T_ALL - TAIL_ROWS) // TILE
    any_spec = pl.BlockSpec(memory_space=pl.ANY)
    return pl.pallas_call(
        _out_proj_tail_kernel,
        grid=(TAIL_ROWS // TILE,),
        in_specs=[pl.BlockSpec((TILE, 2 * D_SSD), lambda i: (i, 0)),
                  pl.BlockSpec((2 * D_SSD, D_MODEL), lambda i: (0, 0), pipeline_mode=pl.Buffered(1)),
                  pl.BlockSpec((TILE, D_MODEL), lambda i: (blk0 + i, 0)),
                  pl.BlockSpec((None, 1, D_MODEL), lambda i: (layer, 0, 0)),
                  any_spec, any_spec, any_spec],
        out_specs=[pl.BlockSpec((TILE, D_MODEL), lambda i: (blk0 + i, 0)),
                   pl.BlockSpec((TILE, D_MODEL), lambda i: (blk0 + i, 0)),
                   pl.BlockSpec((None, TILE, LANES), lambda i: (0, blk0 + i, 0))],
        out_shape=[jax.ShapeDtypeStruct((T_ALL, D_MODEL), F32),
                   jax.ShapeDtypeStruct((T_ALL, D_MODEL), BF16),
                   jax.ShapeDtypeStruct((1, T_ALL, LANES), F32)],
        input_output_aliases={4: 0, 5: 1, 6: 2},
        compiler_params=_cparams(1),
        name="out_proj_tail",
    )(ytail, w_out_bf, h, g_next, h_new, hg, ss)


def _sample_in_specs():
    base = T_PROMPT // TILE
    return [pl.BlockSpec((TILE, 2048), lambda t: (base + t, 0)),
            pl.BlockSpec((TILE, 2048), lambda t: (base + t, 1)),
            pl.BlockSpec((TILE, 2048), lambda t: (base + t, 2)),
            pl.BlockSpec((TILE, 2048), lambda t: (base + t, 3)),
            pl.BlockSpec((TILE, BC_W), lambda t: (base + t, 8)),
            pl.BlockSpec((TILE, LANES), lambda t: (base + t, 0))]


def _hist_specs(layer):
    return [pl.BlockSpec((None, CONV_W - 1, TILE // DEC_SEQ, D_SSD + BC_W), lambda t: (layer, 0, t, 0))]


def _sample_scratch():
    return [pltpu.VMEM((TILE + 8, D_SSD), F32), pltpu.VMEM((TILE + 8, BC_W), F32),
            pltpu.VMEM((TILE, D_SSD), F32),
            pltpu.VMEM((CONV_W - 1, (D_SSD + BC_W) // LANES, TILE, LANES), F32)]


def _mix_sample_prep(proj, dtraw, hists, params, layer):
    nt = T_SAMPLE // TILE
    return pl.pallas_call(
        functools.partial(_mixer_kernel, mode="sample_prep"),
        grid=(nt,),
        in_specs=_sample_in_specs() + _hist_specs(layer) + _param_specs(layer),
        out_specs=[pl.BlockSpec((D_SSD, TILE), lambda t: (0, t)),
                   pl.BlockSpec((TILE, GROUPS * STATE), lambda t: (t, 0)),
                   pl.BlockSpec((GROUPS * STATE, TILE), lambda t: (0, t)),
                   pl.BlockSpec((TILE, LANES), lambda t: (t, 0))],
        out_shape=[jax.ShapeDtypeStruct((D_SSD, T_SAMPLE), BF16),
                   jax.ShapeDtypeStruct((T_SAMPLE, GROUPS * STATE), F32),
                   jax.ShapeDtypeStruct((GROUPS * STATE, T_SAMPLE), F32),
                   jax.ShapeDtypeStruct((T_SAMPLE, LANES), F32)],
        scratch_shapes=_sample_scratch(),
        compiler_params=_cparams(1),
        name="mix_sample_prep",
    )(proj, proj, proj, proj, proj, dtraw, hists, *params)


def _mix_sample_final(proj, dtraw, hists, yofft, ybuf, params, layer):
    nt = T_SAMPLE // TILE
    base = 1
    n_in = 6 + 1 + 1
    return pl.pallas_call(
        functools.partial(_mixer_kernel, mode="sample_final"),
        grid=(nt,),
        in_specs=(_sample_in_specs() + _hist_specs(layer)
                  + [pl.BlockSpec((D_SSD, TILE), lambda t: (0, t)), pl.BlockSpec(memory_space=pl.ANY)]
                  + _param_specs(layer)),
        out_specs=[pl.BlockSpec((TILE, 2 * D_SSD), lambda t: (base + t, 0)),
                   pl.BlockSpec((TILE, D_CM), lambda t: (t, 0))],
        out_shape=[jax.ShapeDtypeStruct((TAIL_ROWS, 2 * D_SSD), BF16),
                   jax.ShapeDtypeStruct((T_SAMPLE, D_CM), F32)],
        input_output_aliases={n_in: 0},
        scratch_shapes=_sample_scratch(),
        compiler_params=_cparams(1),
        name="mix_sample_final",
    )(proj, proj, proj, proj, proj, dtraw, hists, yofft, ybuf, *params)


def _state_kernel_aliased(dec_ref, st_ref, xst_ref, bm_ref, cmt_ref, prev_ref, new_ref, yt_ref):
    del prev_ref
    _state_kernel(dec_ref, st_ref, xst_ref, bm_ref, cmt_ref, new_ref, yt_ref)


def _sample_state(dec, state, layer, xst, bm, cmt, new_all):
    per_tile = SEQS_PER_TILE // STATE_SEQS
    in_specs = [pl.BlockSpec(memory_space=pltpu.SMEM),
                pl.BlockSpec((None, STATE_SEQS, D_SSD, STATE), lambda b: (layer, b, 0, 0)),
                pl.BlockSpec((D_SSD, TILE), lambda b: (0, b // per_tile)),
                pl.BlockSpec((TILE, GROUPS * STATE), lambda b: (b // per_tile, 0)),
                pl.BlockSpec((GROUPS * STATE, TILE), lambda b: (0, b // per_tile))]
    args = [dec, state, xst, bm, cmt]
    aliases = {}
    body = _state_kernel
    if new_all is not None:
        in_specs.append(pl.BlockSpec(memory_space=pl.ANY))
        args.append(new_all)
        aliases = {5: 0}
        body = _state_kernel_aliased
    return pl.pallas_call(
        body,
        grid=(DEC_BATCH // STATE_SEQS,),
        in_specs=in_specs,
        out_specs=[pl.BlockSpec((None, STATE_SEQS, D_SSD, STATE), lambda b: (layer, b, 0, 0)),
                   pl.BlockSpec((D_SSD, TILE), lambda b: (0, b // per_tile))],
        out_shape=[jax.ShapeDtypeStruct((DEPTH, DEC_BATCH, D_SSD, STATE), F32),
                   jax.ShapeDtypeStruct((D_SSD, T_SAMPLE), F32)],
        input_output_aliases=aliases,
        compiler_params=_cparams(1),
        name="sample_state",
    )(*args)


def _expansion_matrix():
    e = np.zeros((2 * LANES, D_SSD), np.float32)
    for h in range(HEADS):
        e[h, h * HEAD_DIM:(h + 1) * HEAD_DIM] = 1.0
        e[LANES + h, h * HEAD_DIM:(h + 1) * HEAD_DIM] = 1.0
    return jnp.asarray(e, BF16)


def kernel(x_prompt, x_sample, state_ssm, state_conv, p_prompt, p_sample, g_mix, w_in, conv_w, conv_b, dt_bias, a_log, d_skip, g_ssd, g_cm, w_s, b_s, w_out, g_ffn, w_gate, w_up, w_down, g_pg, w_pg, w_ple, g_final):
    h = jnp.concatenate([x_prompt.reshape(T_PROMPT, D_MODEL), x_sample.reshape(T_SAMPLE, D_MODEL)], axis=0)
    p_all = jnp.concatenate([p_prompt.reshape(DEPTH, T_PROMPT, D_PLE), p_sample.reshape(DEPTH, T_SAMPLE, D_PLE)], axis=1)
    state_all = state_ssm.reshape(DEPTH, DEC_BATCH, D_SSD, STATE)
    w_in_t = jnp.transpose(w_in, (0, 2, 1))
    pad_heads = ((0, 0), (0, 0), (0, LANES - HEADS))
    reps = TILE // DEC_SEQ
    params_common = (conv_w[:, :, :D_SSD], conv_w[:, :, D_SSD:], conv_b[:, None, :D_SSD], conv_b[:, None, D_SSD:],
                     jnp.pad(dt_bias[:, None, :], pad_heads), jnp.pad(a_log[:, None, :], pad_heads),
                     jnp.repeat(d_skip, HEAD_DIM, axis=1)[:, None, :], g_ssd[:, None, :], g_cm[:, None, :],
                     _expansion_matrix())
    params_p = params_common + (w_s, jnp.repeat(jnp.swapaxes(b_s, 1, 2), LANES, axis=2))
    params_s = params_common + (
        jnp.tile(w_s[:, :, :DEC_SEQ, :DEC_SEQ], (1, 1, reps, reps)),
        jnp.tile(jnp.repeat(jnp.swapaxes(b_s[:, :, :DEC_SEQ], 1, 2), LANES, axis=2), (1, reps, 1)))
    hists = jnp.transpose(state_conv, (0, 2, 1, 3))
    ssm_p, conv_p, conv_s, v_s = [], [], [], []
    ssm_s_all = None
    g_ffn3, g_pg3 = g_ffn[:, None, :], g_pg[:, None, :]
    hg, ss = _norm_stats(h, g_mix[0])
    for i in range(DEPTH):
        proj = _in_proj(hg, ss, w_in_t, i, 1088)
        dtraw = _in_proj_dt(hg, ss, w_in_t, i, 1088)

        w_out_bf = _cast_bf16(w_out, i)
        h_new, hg, ss, ssm_new_p, tail_p, ytail = _mix_prompt(proj, dtraw, params_p, w_out_bf, h, g_ffn3, i)
        xst, bmat, cmt, ea = _mix_sample_prep(proj, dtraw, hists, params_s, i)
        dec = ea[DEC_SEQ - 1::DEC_SEQ, :HEADS]
        ssm_s_all, yofft = _sample_state(dec, state_all, i, xst, bmat, cmt, ssm_s_all)
        ytail, vn_s = _mix_sample_final(proj, dtraw, hists, yofft, ytail, params_s, i)
        h, hg, ss = _out_proj_tail(ytail, w_out_bf, h, g_ffn3, i, h_new, hg, ss)
        act = _ffn_up(hg, ss, w_gate, w_up, i, 1088, 512)
        h, hg, ss = _mm_res(act, w_down, i, h, g_pg3, 544, 512, "ffn_down")
        if i + 1 < DEPTH:
            h, hg, ss = _ple(hg, ss, p_all, h, w_pg, w_ple, g_mix[i + 1].reshape(1, -1), i, 544, 1024)
        else:
            h, = _ple(hg, ss, p_all, h, w_pg, w_ple, None, i, 544, 1024)

        xbc_s = jnp.concatenate([lax.slice(proj, (T_PROMPT, D_SSD), (T_ALL, 2 * D_SSD)),
                                 lax.slice(proj, (T_PROMPT, 4 * 2048), (T_ALL, 4 * 2048 + BC_W))], axis=1)
        ssm_p.append(ssm_new_p.reshape(BATCH, HEADS, HEAD_DIM, STATE))
        conv_p.append(tail_p[:, -(CONV_W - 1):])
        conv_s.append(xbc_s.reshape(DEC_BATCH, DEC_SEQ, -1)[:, -(CONV_W - 1):])
        v_s.append(vn_s.reshape(DEC_BATCH, DEC_SEQ, D_CM))
    out_p = _rms(h, g_final, F32, tm=T_SAMPLE, rows=T_PROMPT)
    out_s = _rms(h, g_final, F32, tm=T_SAMPLE, row0=T_PROMPT, rows=T_SAMPLE)
    return (out_p.reshape(BATCH, SEQ, D_MODEL), out_s.reshape(DEC_BATCH, DEC_SEQ, D_MODEL),
            jnp.stack(ssm_p), jnp.stack(conv_p),
            ssm_s_all.reshape(DEPTH, DEC_BATCH, HEADS, HEAD_DIM, STATE), jnp.stack(conv_s), jnp.stack(v_s))
```

```python
import functools

import jax
import jax.numpy as jnp
import numpy as np
from jax import lax
from jax.experimental import pallas as pl
from jax.experimental.pallas import tpu as pltpu

F32 = jnp.float32
BF16 = jnp.bfloat16

D_MODEL = 2048
BATCH = 4
SEQ = 2048
DEPTH = 4
DEC_BATCH = 128
DEC_SEQ = 4
D_SSD = 2048
HEAD_DIM = 64
HEADS = 32
GROUPS = 4
HEADS_PER_GROUP = HEADS // GROUPS
GROUP_W = D_SSD // GROUPS
STATE = 128
CONV_W = 4
BC_W = 2 * GROUPS * STATE
D_CM = 2048
CM_GROUPS = 16
D_PLE = 256
EPS = 1e-6

T_PROMPT = BATCH * SEQ
T_SAMPLE = DEC_BATCH * DEC_SEQ
T_ALL = T_PROMPT + T_SAMPLE
TILE = 128
PROMPT_TILES = SEQ // TILE
TAIL_ROWS = TILE + T_SAMPLE
OUT_CHUNK = 256
LANES = 128

VMEM_LIMIT = 56 * 1024 * 1024

IN_TM = 1088
FFN_UP_TILE = (1088, 512)
FFN_DOWN_TILE = (544, 512)
PLE_TILE = (544, 1024)
NORM_TM = 544


def _cparams(n_axes):
    return pltpu.CompilerParams(dimension_semantics=("arbitrary",) * n_axes,
                                vmem_limit_bytes=VMEM_LIMIT)


def _rms_cast_kernel(x_ref, g_ref, o_ref):
    x = x_ref[...]
    y = x * lax.rsqrt(jnp.mean(x * x, axis=-1, keepdims=True) + EPS)
    o_ref[...] = (y * g_ref[...]).astype(o_ref.dtype)


def _rms(x, g, out_dtype, tm=NORM_TM, row0=0, rows=None):
    d = x.shape[1]
    m = x.shape[0] if rows is None else rows
    blk0 = row0 // tm
    return pl.pallas_call(
        _rms_cast_kernel,
        grid=(m // tm,),
        in_specs=[pl.BlockSpec((tm, d), lambda i: (blk0 + i, 0)),
                  pl.BlockSpec((1, d), lambda i: (0, 0))],
        out_specs=pl.BlockSpec((tm, d), lambda i: (i, 0)),
        out_shape=jax.ShapeDtypeStruct((m, d), out_dtype),
        compiler_params=_cparams(1),
        name="rms_" + jnp.dtype(out_dtype).name,
    )(x, g.reshape(1, d))


IN_TN = 1024
IN_ALIGNED_TILES = (2 * D_SSD + BC_W) // IN_TN
IN_SHIFT = HEADS
IN_TILES = IN_ALIGNED_TILES + 2 * D_CM // IN_TN


_NT = (((1,), (1,)), ((), ()))


def _in_proj_kernel(x_ref, ss_ref, w_ref, wnext_ref, o_ref, wb_ref):
    j = pl.program_id(0)
    first = pl.program_id(1) == 0

    @pl.when(first & (j < IN_ALIGNED_TILES))
    def _():
        wb_ref[...] = w_ref[...].astype(BF16)

    @pl.when(first & (j >= IN_ALIGNED_TILES))
    def _():
        wb_ref[:IN_TN - IN_SHIFT, :] = w_ref[IN_SHIFT:, :].astype(BF16)
        wb_ref[IN_TN - IN_SHIFT:, :] = wnext_ref[...].astype(BF16)

    def project():
        return _row_scale(ss_ref) * lax.dot_general(x_ref[...], wb_ref[...], _NT, preferred_element_type=F32)

    @pl.when(j < D_SSD // IN_TN)
    def _():
        o_ref[...] = jax.nn.silu(project())

    @pl.when((j >= D_SSD // IN_TN) & (j < IN_ALIGNED_TILES))
    def _():
        o_ref[...] = project()

    @pl.when(j >= IN_ALIGNED_TILES)
    def _():
        o_ref[...] = jax.nn.gelu(project())


def _in_proj(x, ss, w_in_t, layer, tm):
    m, k = x.shape
    per_shift = IN_TN // IN_SHIFT

    def out_col(j):
        return jnp.where(j < IN_ALIGNED_TILES - 1, j, jnp.where(j == IN_ALIGNED_TILES - 1, IN_TILES - 1, j - 1))

    return pl.pallas_call(
        _in_proj_kernel,
        grid=(IN_TILES, m // tm),
        in_specs=[pl.BlockSpec((tm, k), lambda j, i: (i, 0)),
                  _ss_spec(ss.shape[0], tm),
                  pl.BlockSpec((None, IN_TN, k), lambda j, i: (layer, j, 0)),
                  pl.BlockSpec((None, IN_SHIFT, k),
                               lambda j, i: (layer, per_shift * (jnp.maximum(j, IN_ALIGNED_TILES - 1) + 1), 0))],
        out_specs=pl.BlockSpec((tm, IN_TN), lambda j, i: (i, out_col(j))),
        out_shape=jax.ShapeDtypeStruct((m, IN_TILES * IN_TN), F32),
        scratch_shapes=[pltpu.VMEM((IN_TN, k), BF16)],
        compiler_params=_cparams(2),
        name="in_proj",
    )(x, ss, w_in_t, w_in_t)


def _in_proj_dt_kernel(x_ref, ss_ref, w_ref, o_ref):
    o_ref[...] = _row_scale(ss_ref) * lax.dot_general(x_ref[...], w_ref[...].astype(BF16), _NT,
                                                      preferred_element_type=F32)


def _in_proj_dt(x, ss, w_in_t, layer, tm):
    m, k = x.shape
    blk = (2 * D_SSD + BC_W) // LANES
    parts = ss.shape[0]
    return pl.pallas_call(
        _in_proj_dt_kernel,
        grid=(m // tm,),
        in_specs=[pl.BlockSpec((tm, k), lambda i: (i, 0)),
                  pl.BlockSpec((parts, tm, LANES), lambda i: (0, i, 0)),
                  pl.BlockSpec((None, LANES, k), lambda i: (layer, blk, 0))],
        out_specs=pl.BlockSpec((tm, LANES), lambda i: (i, 0)),
        out_shape=jax.ShapeDtypeStruct((m, LANES), F32),
        compiler_params=_cparams(1),
        name="in_proj_dt",
    )(x, ss, w_in_t)


def _row_scale(ss_ref):
    tot = ss_ref[0]
    for part in range(1, ss_ref.shape[0]):
        tot = tot + ss_ref[part]
    return lax.rsqrt(tot[:, :1] * (1.0 / D_MODEL) + EPS)


def _emit_norm_inputs(h_new, g_ref, hg_ref, ss_ref):
    hg_ref[...] = (h_new * g_ref[...]).astype(hg_ref.dtype)
    ss_ref[...] = jnp.broadcast_to(jnp.sum(h_new * h_new, axis=1, keepdims=True), ss_ref.shape)


def _norm_outputs(m, n, tm, tn):
    specs = [pl.BlockSpec((tm, tn), lambda j, i: (i, j)), pl.BlockSpec((None, tm, LANES), lambda j, i: (j, i, 0))]
    shapes = [jax.ShapeDtypeStruct((m, n), BF16), jax.ShapeDtypeStruct((n // tn, m, LANES), F32)]
    return specs, shapes


def _ss_spec(parts, tm):
    return pl.BlockSpec((parts, tm, LANES), lambda j, i: (0, i, 0))


def _norm_stats_kernel(x_ref, g_ref, hg_ref, ss_ref):
    x = x_ref[...]
    _emit_norm_inputs(x, g_ref, hg_ref, ss_ref)


def _norm_stats(x, g, tm=NORM_TM):
    m, d = x.shape
    return pl.pallas_call(
        _norm_stats_kernel,
        grid=(m // tm,),
        in_specs=[pl.BlockSpec((tm, d), lambda i: (i, 0)), pl.BlockSpec((1, d), lambda i: (0, 0))],
        out_specs=[pl.BlockSpec((tm, d), lambda i: (i, 0)), pl.BlockSpec((None, tm, LANES), lambda i: (0, i, 0))],
        out_shape=[jax.ShapeDtypeStruct((m, d), BF16), jax.ShapeDtypeStruct((1, m, LANES), F32)],
        compiler_params=_cparams(1),
        name="norm_stats",
    )(x, g.reshape(1, d))


def _mm_res_kernel(x_ref, w_ref, h_ref, g_ref, o_ref, hg_ref, ss_ref, wb_ref):
    @pl.when(pl.program_id(1) == 0)
    def _():
        wb_ref[...] = w_ref[...].astype(BF16)

    h_new = h_ref[...] + jnp.dot(x_ref[...], wb_ref[...], preferred_element_type=F32)
    o_ref[...] = h_new
    _emit_norm_inputs(h_new, g_ref, hg_ref, ss_ref)


def _mm_res(x, w, layer, h, g_next, tm, tn, name, weight_buffers=2):
    m, k = x.shape
    n = w.shape[2]
    nspecs, nshapes = _norm_outputs(m, n, tm, tn)
    return pl.pallas_call(
        _mm_res_kernel,
        grid=(n // tn, m // tm),
        in_specs=[pl.BlockSpec((tm, k), lambda j, i: (i, 0)),
                  pl.BlockSpec((None, k, tn), lambda j, i: (layer, 0, j),
                               pipeline_mode=pl.Buffered(weight_buffers)),
                  pl.BlockSpec((tm, tn), lambda j, i: (i, j)),
                  pl.BlockSpec((None, 1, tn), lambda j, i: (layer, 0, j))],
        out_specs=[pl.BlockSpec((tm, tn), lambda j, i: (i, j))] + nspecs,
        out_shape=[jax.ShapeDtypeStruct((m, n), F32)] + nshapes,
        scratch_shapes=[pltpu.VMEM((k, tn), BF16)],
        compiler_params=_cparams(2),
        name=name,
    )(x, w, h, g_next)


def _ffn_up_kernel(x_ref, ss_ref, wg_ref, wu_ref, o_ref, wgb_ref, wub_ref):
    @pl.when(pl.program_id(1) == 0)
    def _():
        wgb_ref[...] = wg_ref[...].astype(BF16)
        wub_ref[...] = wu_ref[...].astype(BF16)

    x = x_ref[...]
    r = _row_scale(ss_ref)
    gate = r * jnp.dot(x, wgb_ref[...], preferred_element_type=F32)
    up = r * jnp.dot(x, wub_ref[...], preferred_element_type=F32)
    o_ref[...] = (jax.nn.silu(gate) * up).astype(o_ref.dtype)


def _ffn_up(x, ss, wg, wu, layer, tm, tn):
    m, k = x.shape
    n = wg.shape[2]
    return pl.pallas_call(
        _ffn_up_kernel,
        grid=(n // tn, m // tm),
        in_specs=[pl.BlockSpec((tm, k), lambda j, i: (i, 0)),
                  _ss_spec(ss.shape[0], tm),
                  pl.BlockSpec((None, k, tn), lambda j, i: (layer, 0, j)),
                  pl.BlockSpec((None, k, tn), lambda j, i: (layer, 0, j))],
        out_specs=pl.BlockSpec((tm, tn), lambda j, i: (i, j)),
        out_shape=jax.ShapeDtypeStruct((m, n), BF16),
        scratch_shapes=[pltpu.VMEM((k, tn), BF16), pltpu.VMEM((k, tn), BF16)],
        compiler_params=_cparams(2),
        name="ffn_up",
    )(x, ss, wg, wu)


def _ple_kernel(*refs, emit_norm):
    n_ref, ss_ref, p_ref, h_ref, wpg_ref, wple_ref = refs[:6]
    if emit_norm:
        g_ref, o_ref, hg_ref, ssn_ref, wpgb_ref, wpleb_ref = refs[6:]
    else:
        o_ref, wpgb_ref, wpleb_ref = refs[6:]

    @pl.when(pl.program_id(1) == 0)
    def _():
        wpgb_ref[...] = wpg_ref[...].astype(BF16)
        wpleb_ref[...] = wple_ref[...].astype(BF16)

    gate = jax.nn.sigmoid(_row_scale(ss_ref) * jnp.dot(n_ref[...], wpgb_ref[...], preferred_element_type=F32))
    emb = jnp.dot(p_ref[...].astype(BF16), wpleb_ref[...], preferred_element_type=F32)
    h_new = h_ref[...] + emb * gate
    o_ref[...] = h_new
    if emit_norm:
        _emit_norm_inputs(h_new, g_ref, hg_ref, ssn_ref)


def _ple(n, ss, p, h, wpg, wple, g_next, layer, tm, tn):
    m, k = n.shape
    kp = p.shape[2]
    nn = wpg.shape[2]
    emit_norm = g_next is not None
    in_specs = [pl.BlockSpec((tm, k), lambda j, i: (i, 0)),
                _ss_spec(ss.shape[0], tm),
                pl.BlockSpec((None, tm, kp), lambda j, i: (layer, i, 0)),
                pl.BlockSpec((tm, tn), lambda j, i: (i, j)),
                pl.BlockSpec((None, k, tn), lambda j, i: (layer, 0, j)),
                pl.BlockSpec((None, kp, tn), lambda j, i: (layer, 0, j))]
    args = [n, ss, p, h, wpg, wple]
    out_specs = [pl.BlockSpec((tm, tn), lambda j, i: (i, j))]
    out_shape = [jax.ShapeDtypeStruct((m, nn), F32)]
    if emit_norm:
        in_specs.append(pl.BlockSpec((1, tn), lambda j, i: (0, j)))
        args.append(g_next)
        nspecs, nshapes = _norm_outputs(m, nn, tm, tn)
        out_specs += nspecs
        out_shape += nshapes
    return pl.pallas_call(
        functools.partial(_ple_kernel, emit_norm=emit_norm),
        grid=(nn // tn, m // tm),
        in_specs=in_specs,
        out_specs=out_specs,
        out_shape=out_shape,
        scratch_shapes=[pltpu.VMEM((k, tn), BF16), pltpu.VMEM((kp, tn), BF16)],
        compiler_params=_cparams(2),
        name="ple",
    )(*args)


def _softplus(x):
    return jnp.maximum(x, 0.0) + jnp.log1p(jnp.exp(-jnp.abs(x)))


def _split_hi_lo(v):
    lane = lax.broadcasted_iota(jnp.int32, v.shape, 1)
    v = jnp.where(lane < HEADS, v, 0.0)
    hi = v.astype(BF16)
    lo = (v - hi.astype(F32)).astype(BF16)
    return jnp.concatenate([hi, lo], axis=1)


def _expand_heads(v_hl, e2_ref, cols):
    return jnp.dot(v_hl, e2_ref[:, cols], preferred_element_type=F32)


def _mixer_kernel(*refs, mode):
    prompt = mode == "prompt"
    prep = mode == "sample_prep"
    it = iter(refs)
    z_ref, x_ref, u_ref, v_ref, bc_ref, dt_ref = (next(it) for _ in range(6))
    if not prompt:
        sc_ref = next(it)
    if mode == "sample_final":
        yofft_ref = next(it)
        next(it)
    (cwx_ref, cwbc_ref, cbx_ref, cbbc_ref, dtb_ref, alog_ref, dsk_ref, gssd_ref, gcm_ref,
     e2_ref, ws_ref, bexp_ref) = (next(it) for _ in range(12))
    if prompt:
        wout_ref, h_ref, gn_ref = next(it), next(it), next(it)
        hn_ref, hg_ref, ss_ref, ssm_ref, tail_ref, ylast_ref = (next(it) for _ in range(6))
        st_ref, extx_ref, extbc_ref, y_ref = (next(it) for _ in range(4))
        chunk = pl.program_id(0) & (PROMPT_TILES - 1)
    elif prep:
        xst_ref, bm_ref, cmt_ref, ea_ref = (next(it) for _ in range(4))
        extx_ref, extbc_ref, hist_ref = (next(it) for _ in range(3))
    else:
        y_ref, vn_ref = next(it), next(it)
        extx_ref, extbc_ref, hist_ref = (next(it) for _ in range(3))

    row = lax.broadcasted_iota(jnp.int32, (TILE, LANES), 0)
    col = lax.broadcasted_iota(jnp.int32, (TILE, LANES), 1)
    if prompt:
        pair_mask = col <= row
    else:
        pair_mask = (col <= row) & ((col >> 2) == (row >> 2))

    if prompt:
        @pl.when(pl.program_id(0) == 0)
        def _():
            y_ref[...] = jnp.zeros_like(y_ref)

        @pl.when(chunk == 0)
        def _():
            st_ref[...] = jnp.zeros_like(st_ref)
            extx_ref[0:8, :] = jnp.zeros((8, D_SSD), F32)
            extbc_ref[0:8, :] = jnp.zeros((8, BC_W), F32)

        sumsq = jnp.zeros((TILE, 1), F32)
        for n0 in range(0, D_MODEL, OUT_CHUNK):
            cs = slice(n0, n0 + OUT_CHUNK)
            h_new = h_ref[:, cs] + jnp.dot(y_ref[...], wout_ref[:, cs], preferred_element_type=F32)
            hn_ref[:, cs] = h_new
            hg_ref[:, cs] = (h_new * gn_ref[:, cs]).astype(hg_ref.dtype)
            sumsq = sumsq + jnp.sum(h_new * h_new, axis=1, keepdims=True)
        ss_ref[...] = jnp.broadcast_to(sumsq, ss_ref.shape)
    else:
        hist_ref[...] = jnp.zeros_like(hist_ref)
        for j in range(1, CONV_W):
            for l in range(j):
                for ct in range((D_SSD + BC_W) // LANES):
                    hist_ref[j - 1, ct, pl.ds(l, TILE // DEC_SEQ, stride=DEC_SEQ), :] = (
                        sc_ref[CONV_W - 1 - j + l, :, ct * LANES:(ct + 1) * LANES])

    def conv(src_ref, ext_ref, w_ref, b_ref, c0, width, hist_c0):
        cols = slice(c0, c0 + width)
        if not prompt:
            ext_ref[0:8, cols] = jnp.zeros((8, width), F32)
        ext_ref[8:8 + TILE, cols] = src_ref[:, cols]
        acc = b_ref[:, cols] + w_ref[CONV_W - 1:CONV_W, cols] * ext_ref[8:8 + TILE, cols]
        rw = lax.broadcasted_iota(jnp.int32, (TILE, width), 0)
        for j in range(1, CONV_W):
            shifted = ext_ref[8 - j:8 - j + TILE, cols]
            if not prompt:
                ct0 = (hist_c0 + c0) // LANES
                hist = jnp.concatenate([hist_ref[j - 1, ct] for ct in range(ct0, ct0 + width // LANES)], axis=1)
                shifted = jnp.where((rw & (DEC_SEQ - 1)) >= j, shifted, hist)
            acc = acc + w_ref[CONV_W - 1 - j:CONV_W - j, cols] * shifted
        if prompt:
            ext_ref[0:8, cols] = ext_ref[TILE:TILE + 8, cols]
        return jax.nn.silu(acc)

    dt = _softplus(dt_ref[...] + dtb_ref[...])
    dta = dt * (-jnp.exp(alog_ref[...]))
    acum = dta
    if prompt:
        for s in (1, 2, 4, 8, 16, 32, 64):
            acum = acum + jnp.where(row >= s, pltpu.roll(acum, s, axis=0), 0.0)
        alast = acum[TILE - 1:TILE, :]
    else:
        pos = row & (DEC_SEQ - 1)
        for s in (1, 2):
            acum = acum + jnp.where(pos >= s, pltpu.roll(acum, s, axis=0), 0.0)
        alast = jnp.where(pos == DEC_SEQ - 1, acum, 0.0)
        alast = alast + jnp.where(pos == DEC_SEQ - 2, pltpu.roll(alast, TILE - 1, axis=0), 0.0)
        alast = alast + jnp.where(pos < DEC_SEQ - 2, pltpu.roll(alast, TILE - 2, axis=0), 0.0)
    ea = jnp.exp(acum)
    dsc = dt * jnp.exp(alast - acum)
    dt_hl, ea_hl, dsc_hl = _split_hi_lo(dt), _split_hi_lo(ea), _split_hi_lo(dsc)
    acum_t = acum.T
    lo_half = col < HEAD_DIM

    for g in range(GROUPS):
        sl = slice(g * GROUP_W, (g + 1) * GROUP_W)
        xs = conv(x_ref, extx_ref, cwx_ref, cbx_ref, g * GROUP_W, GROUP_W, 0)
        bm = conv(bc_ref, extbc_ref, cwbc_ref, cbbc_ref, g * STATE, STATE, D_SSD)
        cm = conv(bc_ref, extbc_ref, cwbc_ref, cbbc_ref, (GROUPS + g) * STATE, STATE, D_SSD)
        xsc = (xs * _expand_heads(dsc_hl, e2_ref, sl)).astype(BF16)
        if prep:
            xst_ref[sl, :] = xsc.T
            bm_ref[:, g * STATE:(g + 1) * STATE] = bm
            cmt_ref[g * STATE:(g + 1) * STATE, :] = cm.T
            continue
        dt_e = _expand_heads(dt_hl, e2_ref, sl)
        ea_e = _expand_heads(ea_hl, e2_ref, sl)

        cb = lax.dot_general(cm.astype(BF16), bm.astype(BF16), _NT, preferred_element_type=F32)
        ydiag = []
        for jj in range(HEADS_PER_GROUP // 2):
            w_pair = []
            for h in (g * HEADS_PER_GROUP + 2 * jj, g * HEADS_PER_GROUP + 2 * jj + 1):
                seg = acum[:, h:h + 1] - acum_t[h:h + 1, :]
                dec = jnp.exp(jnp.where(pair_mask, seg, -jnp.inf))
                w_pair.append((cb * dec).astype(BF16))
            ps = slice(jj * LANES, (jj + 1) * LANES)
            xdt = (xs[:, ps] * dt_e[:, ps]).astype(BF16)
            rhs = jnp.concatenate([jnp.where(lo_half, xdt, jnp.zeros_like(xdt)),
                                   jnp.where(lo_half, jnp.zeros_like(xdt), xdt)], axis=0)
            ydiag.append(jnp.dot(jnp.concatenate(w_pair, axis=1), rhs, preferred_element_type=F32))

        if prompt:
            yoff = jnp.dot(cm.astype(BF16), st_ref[:, sl].astype(BF16), preferred_element_type=F32)
            upd = jnp.dot(bm.T.astype(BF16), xsc, preferred_element_type=F32)
            st_ref[:, sl] = st_ref[:, sl] * ea_e[TILE - 1:TILE, :] + upd
        else:
            yoff = yofft_ref[sl, :].T

        y = jnp.concatenate(ydiag, axis=1) + yoff * ea_e + dsk_ref[:, sl] * xs
        y = y * z_ref[:, sl]
        y = y * lax.rsqrt(jnp.mean(y * y, axis=-1, keepdims=True) + EPS)
        y_ref[:, sl] = (y * gssd_ref[:, sl]).astype(y_ref.dtype)

    if prep:
        ea_ref[...] = ea
        return

    gv = v_ref[...]
    vn = gv * lax.rsqrt(jnp.mean(gv * gv, axis=-1, keepdims=True) + EPS) * gcm_ref[...]
    if not prompt:
        vn_ref[...] = vn
    for g in range(CM_GROUPS):
        sl = slice(g * LANES, (g + 1) * LANES)
        wm = jnp.where(pair_mask, ws_ref[g], 0.0).astype(BF16)
        s = jnp.dot(wm, vn[:, sl].astype(BF16), preferred_element_type=F32) + bexp_ref[:, sl]
        y_ref[:, D_SSD + g * LANES:D_SSD + (g + 1) * LANES] = (u_ref[:, sl] * s).astype(y_ref.dtype)

    if prompt:
        @pl.when(chunk == PROMPT_TILES - 1)
        def _():
            ssm_ref[0] = st_ref[...].T
            tail_ref[0, :, :D_SSD] = extx_ref[0:8, :]
            tail_ref[0, :, D_SSD:] = extbc_ref[0:8, :]

        @pl.when(pl.program_id(0) == pl.num_programs(0) - 1)
        def _():
            ylast_ref[...] = y_ref[...]


STATE_SEQS = 8
SEQS_PER_TILE = TILE // DEC_SEQ


def _state_kernel(dec_ref, st_ref, xst_ref, bm_ref, cmt_ref, new_ref, yt_ref):
    b0 = pl.program_id(0) * STATE_SEQS
    bl0 = b0 % SEQS_PER_TILE

    @pl.when(bl0 == 0)
    def _():
        yt_ref[...] = jnp.zeros_like(yt_ref)

    row = lax.broadcasted_iota(jnp.int32, (TILE, LANES), 0)
    col = lax.broadcasted_iota(jnp.int32, (TILE, LANES), 1)
    for g in range(GROUPS):
        rs = slice(g * GROUP_W, (g + 1) * GROUP_W)
        cmt_g = cmt_ref[g * STATE:(g + 1) * STATE, :]
        bm_g = bm_ref[:, g * STATE:(g + 1) * STATE]
        states, ctms = [], []
        for q in range(STATE_SEQS):
            sg = st_ref[q, rs, :]
            states.append(sg.astype(BF16))
            ctms.append(jnp.where((col >> 2) == bl0 + q, cmt_g, 0.0).astype(BF16))
            bmk = jnp.where((row >> 2) == bl0 + q, bm_g, 0.0).astype(BF16)
            upd = jnp.dot(xst_ref[rs, :], bmk, preferred_element_type=F32)
            for hh in range(HEADS_PER_GROUP):
                hs = slice(hh * HEAD_DIM, (hh + 1) * HEAD_DIM)
                new_ref[q, g * GROUP_W + hh * HEAD_DIM:g * GROUP_W + (hh + 1) * HEAD_DIM, :] = (
                    sg[hs, :] * dec_ref[b0 + q, g * HEADS_PER_GROUP + hh] + upd[hs, :])
        yt_ref[rs, :] += jnp.dot(jnp.concatenate(states, axis=1), jnp.concatenate(ctms, axis=0),
                                 preferred_element_type=F32)


def _const_spec(shape):
    nd = len(shape)
    return pl.BlockSpec(shape, lambda *idx: (0,) * nd)


def _layer_spec(shape, layer):
    nd = len(shape)
    return pl.BlockSpec((None,) + tuple(shape), lambda *idx: (layer,) + (0,) * nd)


def _param_specs(layer):
    return [_layer_spec((CONV_W, D_SSD), layer), _layer_spec((CONV_W, BC_W), layer),
            _layer_spec((1, D_SSD), layer), _layer_spec((1, BC_W), layer),
            _layer_spec((1, LANES), layer), _layer_spec((1, LANES), layer),
            _layer_spec((1, D_SSD), layer), _layer_spec((1, D_SSD), layer), _layer_spec((1, D_CM), layer),
            _const_spec((2 * LANES, D_SSD)), _layer_spec((CM_GROUPS, TILE, TILE), layer),
            _layer_spec((TILE, D_CM), layer)]


def _cast_kernel(w_ref, o_ref):
    o_ref[...] = w_ref[...].astype(o_ref.dtype)


def _cast_bf16(w, layer, tr=512):
    _, k, n = w.shape
    return pl.pallas_call(
        _cast_kernel,
        grid=(k // tr,),
        in_specs=[pl.BlockSpec((None, tr, n), lambda i: (layer, i, 0))],
        out_specs=pl.BlockSpec((tr, n), lambda i: (i, 0)),
        out_shape=jax.ShapeDtypeStruct((k, n), BF16),
        compiler_params=_cparams(1),
        name="cast_bf16",
    )(w)


def _mix_prompt(proj, dtraw, params, w_out_bf, h, g_next, layer):
    tiles = T_PROMPT // TILE

    def prev(t):
        return jnp.maximum(t - 1, 0)

    in_specs = [pl.BlockSpec((TILE, 2048), lambda t: (t, 0)),
                pl.BlockSpec((TILE, 2048), lambda t: (t, 1)),
                pl.BlockSpec((TILE, 2048), lambda t: (t, 2)),
                pl.BlockSpec((TILE, 2048), lambda t: (t, 3)),
                pl.BlockSpec((TILE, BC_W), lambda t: (t, 8)),
                pl.BlockSpec((TILE, LANES), lambda t: (t, 0))]
    in_specs += _param_specs(layer)
    in_specs += [pl.BlockSpec((2 * D_SSD, D_MODEL), lambda t: (0, 0), pipeline_mode=pl.Buffered(1)),
                 pl.BlockSpec((TILE, D_MODEL), lambda t: (prev(t), 0)),
                 pl.BlockSpec((None, 1, D_MODEL), lambda t: (layer, 0, 0))]
    return pl.pallas_call(
        functools.partial(_mixer_kernel, mode="prompt"),
        grid=(tiles,),
        in_specs=in_specs,
        out_specs=[pl.BlockSpec((TILE, D_MODEL), lambda t: (prev(t), 0)),
                   pl.BlockSpec((TILE, D_MODEL), lambda t: (prev(t), 0)),
                   pl.BlockSpec((None, TILE, LANES), lambda t: (0, prev(t), 0)),
                   pl.BlockSpec((1, D_SSD, STATE), lambda t: (t // PROMPT_TILES, 0, 0)),
                   pl.BlockSpec((1, 8, D_SSD + BC_W), lambda t: (t // PROMPT_TILES, 0, 0)),
                   pl.BlockSpec((TILE, 2 * D_SSD), lambda t: (0, 0))],
        out_shape=[jax.ShapeDtypeStruct((T_ALL, D_MODEL), F32),
                   jax.ShapeDtypeStruct((T_ALL, D_MODEL), BF16),
                   jax.ShapeDtypeStruct((1, T_ALL, LANES), F32),
                   jax.ShapeDtypeStruct((BATCH, D_SSD, STATE), F32),
                   jax.ShapeDtypeStruct((BATCH, 8, D_SSD + BC_W), F32),
                   jax.ShapeDtypeStruct((TAIL_ROWS, 2 * D_SSD), BF16)],
        scratch_shapes=[pltpu.VMEM((STATE, D_SSD), F32),
                        pltpu.VMEM((TILE + 8, D_SSD), F32),
                        pltpu.VMEM((TILE + 8, BC_W), F32),
                        pltpu.VMEM((TILE, 2 * D_SSD), BF16)],
        compiler_params=_cparams(1),
        name="mix_prompt",
    )(proj, proj, proj, proj, proj, dtraw, *params, w_out_bf, h, g_next)


def _out_proj_tail_kernel(y_ref, w_ref, h_ref, g_ref, hn_in, hg_in, ss_in, hn_ref, hg_ref, ss_ref):
    del hn_in, hg_in, ss_in
    h_new = h_ref[...] + jnp.dot(y_ref[...], w_ref[...], preferred_element_type=F32)
    hn_ref[...] = h_new
    _emit_norm_inputs(h_new, g_ref, hg_ref, ss_ref)


def _out_proj_tail(ytail, w_out_bf, h, g_next, layer, h_new, hg, ss):
    blk0 = (T_ALL - TAIL_ROWS) // TILE
    any_spec = pl.BlockSpec(memory_space=pl.ANY)
    return pl.pallas_call(
        _out_proj_tail_kernel,
        grid=(TAIL_ROWS // TILE,),
        in_specs=[pl.BlockSpec((TILE, 2 * D_SSD), lambda i: (i, 0)),
                  pl.BlockSpec((2 * D_SSD, D_MODEL), lambda i: (0, 0), pipeline_mode=pl.Buffered(1)),
                  pl.BlockSpec((TILE, D_MODEL), lambda i: (blk0 + i, 0)),
                  pl.BlockSpec((None, 1, D_MODEL), lambda i: (layer, 0, 0)),
                  any_spec, any_spec, any_spec],
        out_specs=[pl.BlockSpec((TILE, D_MODEL), lambda i: (blk0 + i, 0)),
                   pl.BlockSpec((TILE, D_MODEL), lambda i: (blk0 + i, 0)),
                   pl.BlockSpec((None, TILE, LANES), lambda i: (0, blk0 + i, 0))],
        out_shape=[jax.ShapeDtypeStruct((T_ALL, D_MODEL), F32),
                   jax.ShapeDtypeStruct((T_ALL, D_MODEL), BF16),
                   jax.ShapeDtypeStruct((1, T_ALL, LANES), F32)],
        input_output_aliases={4: 0, 5: 1, 6: 2},
        compiler_params=_cparams(1),
        name="out_proj_tail",
    )(ytail, w_out_bf, h, g_next, h_new, hg, ss)


def _sample_in_specs():
    base = T_PROMPT // TILE
    return [pl.BlockSpec((TILE, 2048), lambda t: (base + t, 0)),
            pl.BlockSpec((TILE, 2048), lambda t: (base + t, 1)),
            pl.BlockSpec((TILE, 2048), lambda t: (base + t, 2)),
            pl.BlockSpec((TILE, 2048), lambda t: (base + t, 3)),
            pl.BlockSpec((TILE, BC_W), lambda t: (base + t, 8)),
            pl.BlockSpec((TILE, LANES), lambda t: (base + t, 0))]


def _hist_specs(layer):
    return [pl.BlockSpec((None, CONV_W - 1, TILE // DEC_SEQ, D_SSD + BC_W), lambda t: (layer, 0, t, 0))]


def _sample_scratch():
    return [pltpu.VMEM((TILE + 8, D_SSD), F32), pltpu.VMEM((TILE + 8, BC_W), F32),
            pltpu.VMEM((CONV_W - 1, (D_SSD + BC_W) // LANES, TILE, LANES), F32)]


def _mix_sample_prep(proj, dtraw, hists, params, layer):
    nt = T_SAMPLE // TILE
    return pl.pallas_call(
        functools.partial(_mixer_kernel, mode="sample_prep"),
        grid=(nt,),
        in_specs=_sample_in_specs() + _hist_specs(layer) + _param_specs(layer),
        out_specs=[pl.BlockSpec((D_SSD, TILE), lambda t: (0, t)),
                   pl.BlockSpec((TILE, GROUPS * STATE), lambda t: (t, 0)),
                   pl.BlockSpec((GROUPS * STATE, TILE), lambda t: (0, t)),
                   pl.BlockSpec((TILE, LANES), lambda t: (t, 0))],
        out_shape=[jax.ShapeDtypeStruct((D_SSD, T_SAMPLE), BF16),
                   jax.ShapeDtypeStruct((T_SAMPLE, GROUPS * STATE), F32),
                   jax.ShapeDtypeStruct((GROUPS * STATE, T_SAMPLE), F32),
                   jax.ShapeDtypeStruct((T_SAMPLE, LANES), F32)],
        scratch_shapes=_sample_scratch(),
        compiler_params=_cparams(1),
        name="mix_sample_prep",
    )(proj, proj, proj, proj, proj, dtraw, hists, *params)


def _mix_sample_final(proj, dtraw, hists, yofft, ybuf, params, layer):
    nt = T_SAMPLE // TILE
    base = 1
    n_in = 6 + 1 + 1
    return pl.pallas_call(
        functools.partial(_mixer_kernel, mode="sample_final"),
        grid=(nt,),
        in_specs=(_sample_in_specs() + _hist_specs(layer)
                  + [pl.BlockSpec((D_SSD, TILE), lambda t: (0, t)), pl.BlockSpec(memory_space=pl.ANY)]
                  + _param_specs(layer)),
        out_specs=[pl.BlockSpec((TILE, 2 * D_SSD), lambda t: (base + t, 0)),
                   pl.BlockSpec((TILE, D_CM), lambda t: (t, 0))],
        out_shape=[jax.ShapeDtypeStruct((TAIL_ROWS, 2 * D_SSD), BF16),
                   jax.ShapeDtypeStruct((T_SAMPLE, D_CM), F32)],
        input_output_aliases={n_in: 0},
        scratch_shapes=_sample_scratch(),
        compiler_params=_cparams(1),
        name="mix_sample_final",
    )(proj, proj, proj, proj, proj, dtraw, hists, yofft, ybuf, *params)


def _state_kernel_aliased(dec_ref, st_ref, xst_ref, bm_ref, cmt_ref, prev_ref, new_ref, yt_ref):
    del prev_ref
    _state_kernel(dec_ref, st_ref, xst_ref, bm_ref, cmt_ref, new_ref, yt_ref)


def _sample_state(dec, state, layer, xst, bm, cmt, new_all):
    per_tile = SEQS_PER_TILE // STATE_SEQS
    in_specs = [pl.BlockSpec(memory_space=pltpu.SMEM),
                pl.BlockSpec((None, STATE_SEQS, D_SSD, STATE), lambda b: (layer, b, 0, 0)),
                pl.BlockSpec((D_SSD, TILE), lambda b: (0, b // per_tile)),
                pl.BlockSpec((TILE, GROUPS * STATE), lambda b: (b // per_tile, 0)),
                pl.BlockSpec((GROUPS * STATE, TILE), lambda b: (0, b // per_tile))]
    args = [dec, state, xst, bm, cmt]
    aliases = {}
    body = _state_kernel
    if new_all is not None:
        in_specs.append(pl.BlockSpec(memory_space=pl.ANY))
        args.append(new_all)
        aliases = {5: 0}
        body = _state_kernel_aliased
    return pl.pallas_call(
        body,
        grid=(DEC_BATCH // STATE_SEQS,),
        in_specs=in_specs,
        out_specs=[pl.BlockSpec((None, STATE_SEQS, D_SSD, STATE), lambda b: (layer, b, 0, 0)),
                   pl.BlockSpec((D_SSD, TILE), lambda b: (0, b // per_tile))],
        out_shape=[jax.ShapeDtypeStruct((DEPTH, DEC_BATCH, D_SSD, STATE), F32),
                   jax.ShapeDtypeStruct((D_SSD, T_SAMPLE), F32)],
        input_output_aliases=aliases,
        compiler_params=_cparams(1),
        name="sample_state",
    )(*args)


def _expansion_matrix():
    e = np.zeros((2 * LANES, D_SSD), np.float32)
    for h in range(HEADS):
        e[h, h * HEAD_DIM:(h + 1) * HEAD_DIM] = 1.0
        e[LANES + h, h * HEAD_DIM:(h + 1) * HEAD_DIM] = 1.0
    return jnp.asarray(e, BF16)


def kernel(x_prompt, x_sample, state_ssm, state_conv, p_prompt, p_sample, g_mix, w_in, conv_w, conv_b, dt_bias, a_log, d_skip, g_ssd, g_cm, w_s, b_s, w_out, g_ffn, w_gate, w_up, w_down, g_pg, w_pg, w_ple, g_final):
    h = jnp.concatenate([x_prompt.reshape(T_PROMPT, D_MODEL), x_sample.reshape(T_SAMPLE, D_MODEL)], axis=0)
    p_all = jnp.concatenate([p_prompt.reshape(DEPTH, T_PROMPT, D_PLE), p_sample.reshape(DEPTH, T_SAMPLE, D_PLE)], axis=1)
    state_all = state_ssm.reshape(DEPTH, DEC_BATCH, D_SSD, STATE)
    w_in_t = jnp.transpose(w_in, (0, 2, 1))
    pad_heads = ((0, 0), (0, 0), (0, LANES - HEADS))
    reps = TILE // DEC_SEQ
    params_common = (conv_w[:, :, :D_SSD], conv_w[:, :, D_SSD:], conv_b[:, None, :D_SSD], conv_b[:, None, D_SSD:],
                     jnp.pad(dt_bias[:, None, :], pad_heads), jnp.pad(a_log[:, None, :], pad_heads),
                     jnp.repeat(d_skip, HEAD_DIM, axis=1)[:, None, :], g_ssd[:, None, :], g_cm[:, None, :],
                     _expansion_matrix())
    params_p = params_common + (w_s, jnp.repeat(jnp.swapaxes(b_s, 1, 2), LANES, axis=2))
    params_s = params_common + (
        jnp.tile(w_s[:, :, :DEC_SEQ, :DEC_SEQ], (1, 1, reps, reps)),
        jnp.tile(jnp.repeat(jnp.swapaxes(b_s[:, :, :DEC_SEQ], 1, 2), LANES, axis=2), (1, reps, 1)))
    hists = jnp.transpose(state_conv, (0, 2, 1, 3))
    ssm_p, conv_p, conv_s, v_s = [], [], [], []
    ssm_s_all = None
    g_ffn3, g_pg3 = g_ffn[:, None, :], g_pg[:, None, :]
    hg, ss = _norm_stats(h, g_mix[0])
    for i in range(DEPTH):
        proj = _in_proj(hg, ss, w_in_t, i, IN_TM)
        dtraw = _in_proj_dt(hg, ss, w_in_t, i, IN_TM)

        w_out_bf = _cast_bf16(w_out, i)
        h_new, hg, ss, ssm_new_p, tail_p, ytail = _mix_prompt(proj, dtraw, params_p, w_out_bf, h, g_ffn3, i)
        xst, bmat, cmt, ea = _mix_sample_prep(proj, dtraw, hists, params_s, i)
        dec = ea[DEC_SEQ - 1::DEC_SEQ, :HEADS]
        ssm_s_all, yofft = _sample_state(dec, state_all, i, xst, bmat, cmt, ssm_s_all)
        ytail, vn_s = _mix_sample_final(proj, dtraw, hists, yofft, ytail, params_s, i)
        h, hg, ss = _out_proj_tail(ytail, w_out_bf, h, g_ffn3, i, h_new, hg, ss)
        act = _ffn_up(hg, ss, w_gate, w_up, i, *FFN_UP_TILE)
        h, hg, ss = _mm_res(act, w_down, i, h, g_pg3, *FFN_DOWN_TILE, "ffn_down")
        if i + 1 < DEPTH:
            h, hg, ss = _ple(hg, ss, p_all, h, w_pg, w_ple, g_mix[i + 1].reshape(1, -1), i, *PLE_TILE)
        else:
            h, = _ple(hg, ss, p_all, h, w_pg, w_ple, None, i, *PLE_TILE)

        xbc_s = jnp.concatenate([lax.slice(proj, (T_PROMPT, D_SSD), (T_ALL, 2 * D_SSD)),
                                 lax.slice(proj, (T_PROMPT, 4 * 2048), (T_ALL, 4 * 2048 + BC_W))], axis=1)
        ssm_p.append(ssm_new_p.reshape(BATCH, HEADS, HEAD_DIM, STATE))
        conv_p.append(tail_p[:, -(CONV_W - 1):])
        conv_s.append(xbc_s.reshape(DEC_BATCH, DEC_SEQ, -1)[:, -(CONV_W - 1):])
        v_s.append(vn_s.reshape(DEC_BATCH, DEC_SEQ, D_CM))
    out_p = _rms(h, g_final, F32, tm=T_SAMPLE, rows=T_PROMPT)
    out_s = _rms(h, g_final, F32, tm=T_SAMPLE, row0=T_PROMPT, rows=T_SAMPLE)
    return (out_p.reshape(BATCH, SEQ, D_MODEL), out_s.reshape(DEC_BATCH, DEC_SEQ, D_MODEL),
            jnp.stack(ssm_p), jnp.stack(conv_p),
            ssm_s_all.reshape(DEPTH, DEC_BATCH, HEADS, HEAD_DIM, STATE), jnp.stack(conv_s), jnp.stack(v_s))
```

```python
import functools

import jax
import jax.numpy as jnp
import numpy as np
from jax import lax
from jax.experimental import pallas as pl
from jax.experimental.pallas import tpu as pltpu

F32 = jnp.float32
BF16 = jnp.bfloat16

D_MODEL = 2048
BATCH = 4
SEQ = 2048
DEPTH = 4
DEC_BATCH = 128
DEC_SEQ = 4
D_SSD = 2048
HEAD_DIM = 64
HEADS = 32
GROUPS = 4
HEADS_PER_GROUP = HEADS // GROUPS
GROUP_W = D_SSD // GROUPS
STATE = 128
CONV_W = 4
BC_W = 2 * GROUPS * STATE
D_CM = 2048
CM_GROUPS = 16
D_PLE = 256
EPS = 1e-6

T_PROMPT = BATCH * SEQ
T_SAMPLE = DEC_BATCH * DEC_SEQ
T_ALL = T_PROMPT + T_SAMPLE
TILE = 128
PROMPT_TILES = SEQ // TILE
TAIL_ROWS = TILE + T_SAMPLE
OUT_CHUNK = 256
LANES = 128

VMEM_LIMIT = 56 * 1024 * 1024

IN_TM = 1088
FFN_UP_TILE = (1088, 512)
FFN_DOWN_TILE = (544, 512)
PLE_TILE = (544, 1024)
NORM_TM = 544


def _cparams(n_axes):
    return pltpu.CompilerParams(dimension_semantics=("arbitrary",) * n_axes,
                                vmem_limit_bytes=VMEM_LIMIT)


def _rms_cast_kernel(x_ref, g_ref, o_ref):
    x = x_ref[...]
    y = x * lax.rsqrt(jnp.mean(x * x, axis=-1, keepdims=True) + EPS)
    o_ref[...] = (y * g_ref[...]).astype(o_ref.dtype)


def _rms(x, g, out_dtype, tm=NORM_TM, row0=0, rows=None):
    d = x.shape[1]
    m = x.shape[0] if rows is None else rows
    blk0 = row0 // tm
    return pl.pallas_call(
        _rms_cast_kernel,
        grid=(m // tm,),
        in_specs=[pl.BlockSpec((tm, d), lambda i: (blk0 + i, 0)),
                  pl.BlockSpec((1, d), lambda i: (0, 0))],
        out_specs=pl.BlockSpec((tm, d), lambda i: (i, 0)),
        out_shape=jax.ShapeDtypeStruct((m, d), out_dtype),
        compiler_params=_cparams(1),
        name="rms_" + jnp.dtype(out_dtype).name,
    )(x, g.reshape(1, d))


IN_TN = 1024
IN_ALIGNED_TILES = (2 * D_SSD + BC_W) // IN_TN
IN_SHIFT = HEADS
IN_TILES = IN_ALIGNED_TILES + 2 * D_CM // IN_TN


_NT = (((1,), (1,)), ((), ()))


def _in_proj_kernel(x_ref, ss_ref, w_ref, wnext_ref, o_ref, wb_ref):
    j = pl.program_id(0)
    first = pl.program_id(1) == 0

    @pl.when(first & (j < IN_ALIGNED_TILES))
    def _():
        wb_ref[...] = w_ref[...].astype(BF16)

    @pl.when(first & (j >= IN_ALIGNED_TILES))
    def _():
        wb_ref[:IN_TN - IN_SHIFT, :] = w_ref[IN_SHIFT:, :].astype(BF16)
        wb_ref[IN_TN - IN_SHIFT:, :] = wnext_ref[...].astype(BF16)

    def project():
        return _row_scale(ss_ref) * lax.dot_general(x_ref[...], wb_ref[...], _NT, preferred_element_type=F32)

    @pl.when(j < D_SSD // IN_TN)
    def _():
        o_ref[...] = jax.nn.silu(project())

    @pl.when((j >= D_SSD // IN_TN) & (j < IN_ALIGNED_TILES))
    def _():
        o_ref[...] = project()

    @pl.when(j >= IN_ALIGNED_TILES)
    def _():
        o_ref[...] = jax.nn.gelu(project())


def _in_proj(x, ss, w_in_t, layer, tm):
    m, k = x.shape
    per_shift = IN_TN // IN_SHIFT

    def out_col(j):
        return jnp.where(j < IN_ALIGNED_TILES - 1, j, jnp.where(j == IN_ALIGNED_TILES - 1, IN_TILES - 1, j - 1))

    return pl.pallas_call(
        _in_proj_kernel,
        grid=(IN_TILES, m // tm),
        in_specs=[pl.BlockSpec((tm, k), lambda j, i: (i, 0)),
                  _ss_spec(ss.shape[0], tm),
                  pl.BlockSpec((None, IN_TN, k), lambda j, i: (layer, j, 0)),
                  pl.BlockSpec((None, IN_SHIFT, k),
                               lambda j, i: (layer, per_shift * (jnp.maximum(j, IN_ALIGNED_TILES - 1) + 1), 0))],
        out_specs=pl.BlockSpec((tm, IN_TN), lambda j, i: (i, out_col(j))),
        out_shape=jax.ShapeDtypeStruct((m, IN_TILES * IN_TN), F32),
        scratch_shapes=[pltpu.VMEM((IN_TN, k), BF16)],
        compiler_params=_cparams(2),
        name="in_proj",
    )(x, ss, w_in_t, w_in_t)


def _in_proj_dt_kernel(x_ref, ss_ref, w_ref, o_ref):
    o_ref[...] = _row_scale(ss_ref) * lax.dot_general(x_ref[...], w_ref[...].astype(BF16), _NT,
                                                      preferred_element_type=F32)


def _in_proj_dt(x, ss, w_in_t, layer, tm):
    m, k = x.shape
    blk = (2 * D_SSD + BC_W) // LANES
    parts = ss.shape[0]
    return pl.pallas_call(
        _in_proj_dt_kernel,
        grid=(m // tm,),
        in_specs=[pl.BlockSpec((tm, k), lambda i: (i, 0)),
                  pl.BlockSpec((parts, tm, LANES), lambda i: (0, i, 0)),
                  pl.BlockSpec((None, LANES, k), lambda i: (layer, blk, 0))],
        out_specs=pl.BlockSpec((tm, LANES), lambda i: (i, 0)),
        out_shape=jax.ShapeDtypeStruct((m, LANES), F32),
        compiler_params=_cparams(1),
        name="in_proj_dt",
    )(x, ss, w_in_t)


def _row_scale(ss_ref):
    tot = ss_ref[0]
    for part in range(1, ss_ref.shape[0]):
        tot = tot + ss_ref[part]
    return lax.rsqrt(tot[:, :1] * (1.0 / D_MODEL) + EPS)


def _emit_norm_inputs(h_new, g_ref, hg_ref, ss_ref):
    hg_ref[...] = (h_new * g_ref[...]).astype(hg_ref.dtype)
    ss_ref[...] = jnp.broadcast_to(jnp.sum(h_new * h_new, axis=1, keepdims=True), ss_ref.shape)


def _norm_outputs(m, n, tm, tn):
    specs = [pl.BlockSpec((tm, tn), lambda j, i: (i, j)), pl.BlockSpec((None, tm, LANES), lambda j, i: (j, i, 0))]
    shapes = [jax.ShapeDtypeStruct((m, n), BF16), jax.ShapeDtypeStruct((n // tn, m, LANES), F32)]
    return specs, shapes


def _ss_spec(parts, tm):
    return pl.BlockSpec((parts, tm, LANES), lambda j, i: (0, i, 0))


def _norm_stats_kernel(xp_ref, xs_ref, g_ref, hg_ref, ss_ref):
    x = jnp.where(pl.program_id(0) < pl.num_programs(0) - 1, xp_ref[...], xs_ref[...])
    _emit_norm_inputs(x, g_ref, hg_ref, ss_ref)


def _norm_stats(xp, xs, g):
    tm, d = xs.shape
    last = xp.shape[0] // tm - 1
    m = xp.shape[0] + tm
    return pl.pallas_call(
        _norm_stats_kernel,
        grid=(m // tm,),
        in_specs=[pl.BlockSpec((tm, d), lambda i: (jnp.minimum(i, last), 0)),
                  pl.BlockSpec((tm, d), lambda i: (0, 0)),
                  pl.BlockSpec((1, d), lambda i: (0, 0))],
        out_specs=[pl.BlockSpec((tm, d), lambda i: (i, 0)), pl.BlockSpec((None, tm, LANES), lambda i: (0, i, 0))],
        out_shape=[jax.ShapeDtypeStruct((m, d), BF16), jax.ShapeDtypeStruct((1, m, LANES), F32)],
        compiler_params=_cparams(1),
        name="norm_stats",
    )(xp, xs, g.reshape(1, d))


def _mm_res_kernel(x_ref, w_ref, h_ref, g_ref, o_ref, hg_ref, ss_ref, wb_ref):
    @pl.when(pl.program_id(1) == 0)
    def _():
        wb_ref[...] = w_ref[...].astype(BF16)

    h_new = h_ref[...] + jnp.dot(x_ref[...], wb_ref[...], preferred_element_type=F32)
    o_ref[...] = h_new
    _emit_norm_inputs(h_new, g_ref, hg_ref, ss_ref)


def _mm_res(x, w, layer, h, g_next, tm, tn, name, weight_buffers=2):
    m, k = x.shape
    n = w.shape[2]
    nspecs, nshapes = _norm_outputs(m, n, tm, tn)
    return pl.pallas_call(
        _mm_res_kernel,
        grid=(n // tn, m // tm),
        in_specs=[pl.BlockSpec((tm, k), lambda j, i: (i, 0)),
                  pl.BlockSpec((None, k, tn), lambda j, i: (layer, 0, j),
                               pipeline_mode=pl.Buffered(weight_buffers)),
                  pl.BlockSpec((tm, tn), lambda j, i: (i, j)),
                  pl.BlockSpec((None, 1, tn), lambda j, i: (layer, 0, j))],
        out_specs=[pl.BlockSpec((tm, tn), lambda j, i: (i, j))] + nspecs,
        out_shape=[jax.ShapeDtypeStruct((m, n), F32)] + nshapes,
        scratch_shapes=[pltpu.VMEM((k, tn), BF16)],
        compiler_params=_cparams(2),
        name=name,
    )(x, w, h, g_next)


def _ffn_up_kernel(x_ref, ss_ref, wg_ref, wu_ref, o_ref, wgb_ref, wub_ref):
    @pl.when(pl.program_id(1) == 0)
    def _():
        wgb_ref[...] = wg_ref[...].astype(BF16)
        wub_ref[...] = wu_ref[...].astype(BF16)

    x = x_ref[...]
    r = _row_scale(ss_ref)
    gate = r * jnp.dot(x, wgb_ref[...], preferred_element_type=F32)
    up = r * jnp.dot(x, wub_ref[...], preferred_element_type=F32)
    o_ref[...] = (jax.nn.silu(gate) * up).astype(o_ref.dtype)


def _ffn_up(x, ss, wg, wu, layer, tm, tn):
    m, k = x.shape
    n = wg.shape[2]
    return pl.pallas_call(
        _ffn_up_kernel,
        grid=(n // tn, m // tm),
        in_specs=[pl.BlockSpec((tm, k), lambda j, i: (i, 0)),
                  _ss_spec(ss.shape[0], tm),
                  pl.BlockSpec((None, k, tn), lambda j, i: (layer, 0, j)),
                  pl.BlockSpec((None, k, tn), lambda j, i: (layer, 0, j))],
        out_specs=pl.BlockSpec((tm, tn), lambda j, i: (i, j)),
        out_shape=jax.ShapeDtypeStruct((m, n), BF16),
        scratch_shapes=[pltpu.VMEM((k, tn), BF16), pltpu.VMEM((k, tn), BF16)],
        compiler_params=_cparams(2),
        name="ffn_up",
    )(x, ss, wg, wu)


def _ple_kernel(*refs, emit_norm):
    n_ref, ss_ref, p_ref, h_ref, wpg_ref, wple_ref = refs[:6]
    if emit_norm:
        g_ref, o_ref, hg_ref, ssn_ref, wpgb_ref, wpleb_ref = refs[6:]
    else:
        o_ref, wpgb_ref, wpleb_ref = refs[6:]

    @pl.when(pl.program_id(1) == 0)
    def _():
        wpgb_ref[...] = wpg_ref[...].astype(BF16)
        wpleb_ref[...] = wple_ref[...].astype(BF16)

    gate = jax.nn.sigmoid(_row_scale(ss_ref) * jnp.dot(n_ref[...], wpgb_ref[...], preferred_element_type=F32))
    emb = jnp.dot(p_ref[...].astype(BF16), wpleb_ref[...], preferred_element_type=F32)
    h_new = h_ref[...] + emb * gate
    o_ref[...] = h_new
    if emit_norm:
        _emit_norm_inputs(h_new, g_ref, hg_ref, ssn_ref)


def _ple(n, ss, p, h, wpg, wple, g_next, layer, tm, tn):
    m, k = n.shape
    kp = p.shape[2]
    nn = wpg.shape[2]
    emit_norm = g_next is not None
    in_specs = [pl.BlockSpec((tm, k), lambda j, i: (i, 0)),
                _ss_spec(ss.shape[0], tm),
                pl.BlockSpec((None, tm, kp), lambda j, i: (layer, i, 0)),
                pl.BlockSpec((tm, tn), lambda j, i: (i, j)),
                pl.BlockSpec((None, k, tn), lambda j, i: (layer, 0, j)),
                pl.BlockSpec((None, kp, tn), lambda j, i: (layer, 0, j))]
    args = [n, ss, p, h, wpg, wple]
    out_specs = [pl.BlockSpec((tm, tn), lambda j, i: (i, j))]
    out_shape = [jax.ShapeDtypeStruct((m, nn), F32)]
    if emit_norm:
        in_specs.append(pl.BlockSpec((1, tn), lambda j, i: (0, j)))
        args.append(g_next)
        nspecs, nshapes = _norm_outputs(m, nn, tm, tn)
        out_specs += nspecs
        out_shape += nshapes
    return pl.pallas_call(
        functools.partial(_ple_kernel, emit_norm=emit_norm),
        grid=(nn // tn, m // tm),
        in_specs=in_specs,
        out_specs=out_specs,
        out_shape=out_shape,
        scratch_shapes=[pltpu.VMEM((k, tn), BF16), pltpu.VMEM((kp, tn), BF16)],
        compiler_params=_cparams(2),
        name="ple",
    )(*args)


def _softplus(x):
    return jnp.maximum(x, 0.0) + jnp.log1p(jnp.exp(-jnp.abs(x)))


def _split_hi_lo(v):
    lane = lax.broadcasted_iota(jnp.int32, v.shape, 1)
    v = jnp.where(lane < HEADS, v, 0.0)
    hi = v.astype(BF16)
    lo = (v - hi.astype(F32)).astype(BF16)
    return jnp.concatenate([hi, lo], axis=1)


def _expand_heads(v_hl, e2_ref, cols):
    return jnp.dot(v_hl, e2_ref[:, cols], preferred_element_type=F32)


def _mixer_kernel(*refs, mode):
    prompt = mode == "prompt"
    prep = mode == "sample_prep"
    it = iter(refs)
    z_ref, x_ref, u_ref, v_ref, bc_ref, dt_ref = (next(it) for _ in range(6))
    if not prompt:
        sc_ref = next(it)
    if mode == "sample_final":
        yofft_ref = next(it)
        next(it)
    (cwx_ref, cwbc_ref, cbx_ref, cbbc_ref, dtb_ref, alog_ref, dsk_ref, gssd_ref, gcm_ref,
     e2_ref, ws_ref, bexp_ref) = (next(it) for _ in range(12))
    if prompt:
        wout_ref, h_ref, gn_ref = next(it), next(it), next(it)
        hn_ref, hg_ref, ss_ref, ssm_ref, tail_ref, ylast_ref = (next(it) for _ in range(6))
        st_ref, extx_ref, extbc_ref, y_ref = (next(it) for _ in range(4))
        chunk = pl.program_id(0) & (PROMPT_TILES - 1)
    elif prep:
        xst_ref, bm_ref, cmt_ref, ea_ref = (next(it) for _ in range(4))
        extx_ref, extbc_ref, hist_ref = (next(it) for _ in range(3))
    else:
        y_ref, vn_ref = next(it), next(it)
        extx_ref, extbc_ref, hist_ref = (next(it) for _ in range(3))

    row = lax.broadcasted_iota(jnp.int32, (TILE, LANES), 0)
    col = lax.broadcasted_iota(jnp.int32, (TILE, LANES), 1)
    if prompt:
        pair_mask = col <= row
    else:
        pair_mask = (col <= row) & ((col >> 2) == (row >> 2))

    if prompt:
        @pl.when(pl.program_id(0) == 0)
        def _():
            y_ref[...] = jnp.zeros_like(y_ref)

        @pl.when(chunk == 0)
        def _():
            st_ref[...] = jnp.zeros_like(st_ref)
            extx_ref[0:8, :] = jnp.zeros((8, D_SSD), F32)
            extbc_ref[0:8, :] = jnp.zeros((8, BC_W), F32)

        sumsq = jnp.zeros((TILE, 1), F32)
        for n0 in range(0, D_MODEL, OUT_CHUNK):
            cs = slice(n0, n0 + OUT_CHUNK)
            h_new = h_ref[:, cs] + jnp.dot(y_ref[...], wout_ref[:, cs], preferred_element_type=F32)
            hn_ref[:, cs] = h_new
            hg_ref[:, cs] = (h_new * gn_ref[:, cs]).astype(hg_ref.dtype)
            sumsq = sumsq + jnp.sum(h_new * h_new, axis=1, keepdims=True)
        ss_ref[...] = jnp.broadcast_to(sumsq, ss_ref.shape)
    else:
        hist_ref[...] = jnp.zeros_like(hist_ref)
        for j in range(1, CONV_W):
            for l in range(j):
                for ct in range((D_SSD + BC_W) // LANES):
                    hist_ref[j - 1, ct, pl.ds(l, TILE // DEC_SEQ, stride=DEC_SEQ), :] = (
                        sc_ref[CONV_W - 1 - j + l, :, ct * LANES:(ct + 1) * LANES])

    def conv(src_ref, ext_ref, w_ref, b_ref, c0, width, hist_c0):
        cols = slice(c0, c0 + width)
        if not prompt:
            ext_ref[0:8, cols] = jnp.zeros((8, width), F32)
        ext_ref[8:8 + TILE, cols] = src_ref[:, cols]
        acc = b_ref[:, cols] + w_ref[CONV_W - 1:CONV_W, cols] * ext_ref[8:8 + TILE, cols]
        rw = lax.broadcasted_iota(jnp.int32, (TILE, width), 0)
        for j in range(1, CONV_W):
            shifted = ext_ref[8 - j:8 - j + TILE, cols]
            if not prompt:
                ct0 = (hist_c0 + c0) // LANES
                hist = jnp.concatenate([hist_ref[j - 1, ct] for ct in range(ct0, ct0 + width // LANES)], axis=1)
                shifted = jnp.where((rw & (DEC_SEQ - 1)) >= j, shifted, hist)
            acc = acc + w_ref[CONV_W - 1 - j:CONV_W - j, cols] * shifted
        if prompt:
            ext_ref[0:8, cols] = ext_ref[TILE:TILE + 8, cols]
        return jax.nn.silu(acc)

    dt = _softplus(dt_ref[...] + dtb_ref[...])
    dta = dt * (-jnp.exp(alog_ref[...]))
    acum = dta
    if prompt:
        for s in (1, 2, 4, 8, 16, 32, 64):
            acum = acum + jnp.where(row >= s, pltpu.roll(acum, s, axis=0), 0.0)
        alast = acum[TILE - 1:TILE, :]
    else:
        pos = row & (DEC_SEQ - 1)
        for s in (1, 2):
            acum = acum + jnp.where(pos >= s, pltpu.roll(acum, s, axis=0), 0.0)
        alast = jnp.where(pos == DEC_SEQ - 1, acum, 0.0)
        alast = alast + jnp.where(pos == DEC_SEQ - 2, pltpu.roll(alast, TILE - 1, axis=0), 0.0)
        alast = alast + jnp.where(pos < DEC_SEQ - 2, pltpu.roll(alast, TILE - 2, axis=0), 0.0)
    ea = jnp.exp(acum)
    dsc = dt * jnp.exp(alast - acum)
    dt_hl, ea_hl, dsc_hl = _split_hi_lo(dt), _split_hi_lo(ea), _split_hi_lo(dsc)
    acum_t = acum.T
    lo_half = col < HEAD_DIM

    for g in range(GROUPS):
        sl = slice(g * GROUP_W, (g + 1) * GROUP_W)
        xs = conv(x_ref, extx_ref, cwx_ref, cbx_ref, g * GROUP_W, GROUP_W, 0)
        bm = conv(bc_ref, extbc_ref, cwbc_ref, cbbc_ref, g * STATE, STATE, D_SSD)
        cm = conv(bc_ref, extbc_ref, cwbc_ref, cbbc_ref, (GROUPS + g) * STATE, STATE, D_SSD)
        xsc = (xs * _expand_heads(dsc_hl, e2_ref, sl)).astype(BF16)
        if prep:
            xst_ref[sl, :] = xsc.T
            bm_ref[:, g * STATE:(g + 1) * STATE] = bm
            cmt_ref[g * STATE:(g + 1) * STATE, :] = cm.T
            continue
        dt_e = _expand_heads(dt_hl, e2_ref, sl)
        ea_e = _expand_heads(ea_hl, e2_ref, sl)

        cb = lax.dot_general(cm.astype(BF16), bm.astype(BF16), _NT, preferred_element_type=F32)
        ydiag = []
        for jj in range(HEADS_PER_GROUP // 2):
            w_pair = []
            for h in (g * HEADS_PER_GROUP + 2 * jj, g * HEADS_PER_GROUP + 2 * jj + 1):
                seg = acum[:, h:h + 1] - acum_t[h:h + 1, :]
                dec = jnp.exp(jnp.where(pair_mask, seg, -jnp.inf))
                w_pair.append((cb * dec).astype(BF16))
            ps = slice(jj * LANES, (jj + 1) * LANES)
            xdt = (xs[:, ps] * dt_e[:, ps]).astype(BF16)
            rhs = jnp.concatenate([jnp.where(lo_half, xdt, jnp.zeros_like(xdt)),
                                   jnp.where(lo_half, jnp.zeros_like(xdt), xdt)], axis=0)
            ydiag.append(jnp.dot(jnp.concatenate(w_pair, axis=1), rhs, preferred_element_type=F32))

        if prompt:
            yoff = jnp.dot(cm.astype(BF16), st_ref[:, sl].astype(BF16), preferred_element_type=F32)
            upd = jnp.dot(bm.T.astype(BF16), xsc, preferred_element_type=F32)
            st_ref[:, sl] = st_ref[:, sl] * ea_e[TILE - 1:TILE, :] + upd
        else:
            yoff = yofft_ref[sl, :].T

        y = jnp.concatenate(ydiag, axis=1) + yoff * ea_e + dsk_ref[:, sl] * xs
        y = y * z_ref[:, sl]
        y = y * lax.rsqrt(jnp.mean(y * y, axis=-1, keepdims=True) + EPS)
        y_ref[:, sl] = (y * gssd_ref[:, sl]).astype(y_ref.dtype)

    if prep:
        ea_ref[...] = ea
        return

    gv = v_ref[...]
    vn = gv * lax.rsqrt(jnp.mean(gv * gv, axis=-1, keepdims=True) + EPS) * gcm_ref[...]
    if not prompt:
        vn_ref[...] = vn
    for g in range(CM_GROUPS):
        sl = slice(g * LANES, (g + 1) * LANES)
        wm = jnp.where(pair_mask, ws_ref[g], 0.0).astype(BF16)
        s = jnp.dot(wm, vn[:, sl].astype(BF16), preferred_element_type=F32) + bexp_ref[:, sl]
        y_ref[:, D_SSD + g * LANES:D_SSD + (g + 1) * LANES] = (u_ref[:, sl] * s).astype(y_ref.dtype)

    if prompt:
        @pl.when(chunk == PROMPT_TILES - 1)
        def _():
            ssm_ref[0] = st_ref[...].T
            tail_ref[0, :, :D_SSD] = extx_ref[0:8, :]
            tail_ref[0, :, D_SSD:] = extbc_ref[0:8, :]

        @pl.when(pl.program_id(0) == pl.num_programs(0) - 1)
        def _():
            ylast_ref[...] = y_ref[...]


STATE_SEQS = 8
SEQS_PER_TILE = TILE // DEC_SEQ


def _state_kernel(dec_ref, st_ref, xst_ref, bm_ref, cmt_ref, new_ref, yt_ref):
    b0 = pl.program_id(0) * STATE_SEQS
    bl0 = b0 % SEQS_PER_TILE

    @pl.when(bl0 == 0)
    def _():
        yt_ref[...] = jnp.zeros_like(yt_ref)

    row = lax.broadcasted_iota(jnp.int32, (TILE, LANES), 0)
    col = lax.broadcasted_iota(jnp.int32, (TILE, LANES), 1)
    for g in range(GROUPS):
        rs = slice(g * GROUP_W, (g + 1) * GROUP_W)
        cmt_g = cmt_ref[g * STATE:(g + 1) * STATE, :]
        bm_g = bm_ref[:, g * STATE:(g + 1) * STATE]
        states, ctms = [], []
        for q in range(STATE_SEQS):
            sg = st_ref[q, rs, :]
            states.append(sg.astype(BF16))
            ctms.append(jnp.where((col >> 2) == bl0 + q, cmt_g, 0.0).astype(BF16))
            bmk = jnp.where((row >> 2) == bl0 + q, bm_g, 0.0).astype(BF16)
            upd = jnp.dot(xst_ref[rs, :], bmk, preferred_element_type=F32)
            for hh in range(HEADS_PER_GROUP):
                hs = slice(hh * HEAD_DIM, (hh + 1) * HEAD_DIM)
                new_ref[q, g * GROUP_W + hh * HEAD_DIM:g * GROUP_W + (hh + 1) * HEAD_DIM, :] = (
                    sg[hs, :] * dec_ref[b0 + q, g * HEADS_PER_GROUP + hh] + upd[hs, :])
        yt_ref[rs, :] += jnp.dot(jnp.concatenate(states, axis=1), jnp.concatenate(ctms, axis=0),
                                 preferred_element_type=F32)


def _const_spec(shape):
    nd = len(shape)
    return pl.BlockSpec(shape, lambda *idx: (0,) * nd)


def _layer_spec(shape, layer):
    nd = len(shape)
    return pl.BlockSpec((None,) + tuple(shape), lambda *idx: (layer,) + (0,) * nd)


def _param_specs(layer):
    return [_layer_spec((CONV_W, D_SSD), layer), _layer_spec((CONV_W, BC_W), layer),
            _layer_spec((1, D_SSD), layer), _layer_spec((1, BC_W), layer),
            _layer_spec((1, LANES), layer), _layer_spec((1, LANES), layer),
            _layer_spec((1, D_SSD), layer), _layer_spec((1, D_SSD), layer), _layer_spec((1, D_CM), layer),
            _const_spec((2 * LANES, D_SSD)), _layer_spec((CM_GROUPS, TILE, TILE), layer),
            _layer_spec((TILE, D_CM), layer)]


def _cast_kernel(w_ref, o_ref):
    o_ref[...] = w_ref[...].astype(o_ref.dtype)


def _cast_bf16(w, layer, tr=512):
    _, k, n = w.shape
    return pl.pallas_call(
        _cast_kernel,
        grid=(k // tr,),
        in_specs=[pl.BlockSpec((None, tr, n), lambda i: (layer, i, 0))],
        out_specs=pl.BlockSpec((tr, n), lambda i: (i, 0)),
        out_shape=jax.ShapeDtypeStruct((k, n), BF16),
        compiler_params=_cparams(1),
        name="cast_bf16",
    )(w)


def _mix_prompt(proj, dtraw, params, w_out_bf, h, g_next, layer):
    tiles = T_PROMPT // TILE

    def prev(t):
        return jnp.maximum(t - 1, 0)

    in_specs = [pl.BlockSpec((TILE, 2048), lambda t: (t, 0)),
                pl.BlockSpec((TILE, 2048), lambda t: (t, 1)),
                pl.BlockSpec((TILE, 2048), lambda t: (t, 2)),
                pl.BlockSpec((TILE, 2048), lambda t: (t, 3)),
                pl.BlockSpec((TILE, BC_W), lambda t: (t, 8)),
                pl.BlockSpec((TILE, LANES), lambda t: (t, 0))]
    in_specs += _param_specs(layer)
    in_specs += [pl.BlockSpec((2 * D_SSD, D_MODEL), lambda t: (0, 0), pipeline_mode=pl.Buffered(1)),
                 pl.BlockSpec((TILE, D_MODEL), lambda t: (prev(t), 0)),
                 pl.BlockSpec((None, 1, D_MODEL), lambda t: (layer, 0, 0))]
    return pl.pallas_call(
        functools.partial(_mixer_kernel, mode="prompt"),
        grid=(tiles,),
        in_specs=in_specs,
        out_specs=[pl.BlockSpec((TILE, D_MODEL), lambda t: (prev(t), 0)),
                   pl.BlockSpec((TILE, D_MODEL), lambda t: (prev(t), 0)),
                   pl.BlockSpec((None, TILE, LANES), lambda t: (0, prev(t), 0)),
                   pl.BlockSpec((1, D_SSD, STATE), lambda t: (t // PROMPT_TILES, 0, 0)),
                   pl.BlockSpec((1, 8, D_SSD + BC_W), lambda t: (t // PROMPT_TILES, 0, 0)),
                   pl.BlockSpec((TILE, 2 * D_SSD), lambda t: (0, 0))],
        out_shape=[jax.ShapeDtypeStruct((T_ALL, D_MODEL), F32),
                   jax.ShapeDtypeStruct((T_ALL, D_MODEL), BF16),
                   jax.ShapeDtypeStruct((1, T_ALL, LANES), F32),
                   jax.ShapeDtypeStruct((BATCH, D_SSD, STATE), F32),
                   jax.ShapeDtypeStruct((BATCH, 8, D_SSD + BC_W), F32),
                   jax.ShapeDtypeStruct((TAIL_ROWS, 2 * D_SSD), BF16)],
        scratch_shapes=[pltpu.VMEM((STATE, D_SSD), F32),
                        pltpu.VMEM((TILE + 8, D_SSD), F32),
                        pltpu.VMEM((TILE + 8, BC_W), F32),
                        pltpu.VMEM((TILE, 2 * D_SSD), BF16)],
        compiler_params=_cparams(1),
        name="mix_prompt",
    )(proj, proj, proj, proj, proj, dtraw, *params, w_out_bf, h, g_next)


def _out_proj_tail_kernel(y_ref, w_ref, h_ref, g_ref, hn_in, hg_in, ss_in, hn_ref, hg_ref, ss_ref):
    del hn_in, hg_in, ss_in
    h_new = h_ref[...] + jnp.dot(y_ref[...], w_ref[...], preferred_element_type=F32)
    hn_ref[...] = h_new
    _emit_norm_inputs(h_new, g_ref, hg_ref, ss_ref)


def _out_proj_tail(ytail, w_out_bf, h, g_next, layer, h_new, hg, ss):
    blk0 = (T_ALL - TAIL_ROWS) // TILE
    h_blk0 = blk0 if h.shape[0] == T_ALL else 0
    any_spec = pl.BlockSpec(memory_space=pl.ANY)
    return pl.pallas_call(
        _out_proj_tail_kernel,
        grid=(TAIL_ROWS // TILE,),
        in_specs=[pl.BlockSpec((TILE, 2 * D_SSD), lambda i: (i, 0)),
                  pl.BlockSpec((2 * D_SSD, D_MODEL), lambda i: (0, 0), pipeline_mode=pl.Buffered(1)),
                  pl.BlockSpec((TILE, D_MODEL), lambda i: (h_blk0 + i, 0)),
                  pl.BlockSpec((None, 1, D_MODEL), lambda i: (layer, 0, 0)),
                  any_spec, any_spec, any_spec],
        out_specs=[pl.BlockSpec((TILE, D_MODEL), lambda i: (blk0 + i, 0)),
                   pl.BlockSpec((TILE, D_MODEL), lambda i: (blk0 + i, 0)),
                   pl.BlockSpec((None, TILE, LANES), lambda i: (0, blk0 + i, 0))],
        out_shape=[jax.ShapeDtypeStruct((T_ALL, D_MODEL), F32),
                   jax.ShapeDtypeStruct((T_ALL, D_MODEL), BF16),
                   jax.ShapeDtypeStruct((1, T_ALL, LANES), F32)],
        input_output_aliases={4: 0, 5: 1, 6: 2},
        compiler_params=_cparams(1),
        name="out_proj_tail",
    )(ytail, w_out_bf, h, g_next, h_new, hg, ss)


def _sample_in_specs():
    base = T_PROMPT // TILE
    return [pl.BlockSpec((TILE, 2048), lambda t: (base + t, 0)),
            pl.BlockSpec((TILE, 2048), lambda t: (base + t, 1)),
            pl.BlockSpec((TILE, 2048), lambda t: (base + t, 2)),
            pl.BlockSpec((TILE, 2048), lambda t: (base + t, 3)),
            pl.BlockSpec((TILE, BC_W), lambda t: (base + t, 8)),
            pl.BlockSpec((TILE, LANES), lambda t: (base + t, 0))]


def _hist_specs(layer):
    return [pl.BlockSpec((None, CONV_W - 1, TILE // DEC_SEQ, D_SSD + BC_W), lambda t: (layer, 0, t, 0))]


def _sample_scratch():
    return [pltpu.VMEM((TILE + 8, D_SSD), F32), pltpu.VMEM((TILE + 8, BC_W), F32),
            pltpu.VMEM((CONV_W - 1, (D_SSD + BC_W) // LANES, TILE, LANES), F32)]


def _mix_sample_prep(proj, dtraw, hists, params, layer):
    nt = T_SAMPLE // TILE
    return pl.pallas_call(
        functools.partial(_mixer_kernel, mode="sample_prep"),
        grid=(nt,),
        in_specs=_sample_in_specs() + _hist_specs(layer) + _param_specs(layer),
        out_specs=[pl.BlockSpec((D_SSD, TILE), lambda t: (0, t)),
                   pl.BlockSpec((TILE, GROUPS * STATE), lambda t: (t, 0)),
                   pl.BlockSpec((GROUPS * STATE, TILE), lambda t: (0, t)),
                   pl.BlockSpec((TILE, LANES), lambda t: (t, 0))],
        out_shape=[jax.ShapeDtypeStruct((D_SSD, T_SAMPLE), BF16),
                   jax.ShapeDtypeStruct((T_SAMPLE, GROUPS * STATE), F32),
                   jax.ShapeDtypeStruct((GROUPS * STATE, T_SAMPLE), F32),
                   jax.ShapeDtypeStruct((T_SAMPLE, LANES), F32)],
        scratch_shapes=_sample_scratch(),
        compiler_params=_cparams(1),
        name="mix_sample_prep",
    )(proj, proj, proj, proj, proj, dtraw, hists, *params)


def _mix_sample_final(proj, dtraw, hists, yofft, ybuf, params, layer):
    nt = T_SAMPLE // TILE
    base = 1
    n_in = 6 + 1 + 1
    return pl.pallas_call(
        functools.partial(_mixer_kernel, mode="sample_final"),
        grid=(nt,),
        in_specs=(_sample_in_specs() + _hist_specs(layer)
                  + [pl.BlockSpec((D_SSD, TILE), lambda t: (0, t)), pl.BlockSpec(memory_space=pl.ANY)]
                  + _param_specs(layer)),
        out_specs=[pl.BlockSpec((TILE, 2 * D_SSD), lambda t: (base + t, 0)),
                   pl.BlockSpec((TILE, D_CM), lambda t: (t, 0))],
        out_shape=[jax.ShapeDtypeStruct((TAIL_ROWS, 2 * D_SSD), BF16),
                   jax.ShapeDtypeStruct((T_SAMPLE, D_CM), F32)],
        input_output_aliases={n_in: 0},
        scratch_shapes=_sample_scratch(),
        compiler_params=_cparams(1),
        name="mix_sample_final",
    )(proj, proj, proj, proj, proj, dtraw, hists, yofft, ybuf, *params)


def _state_kernel_aliased(dec_ref, st_ref, xst_ref, bm_ref, cmt_ref, prev_ref, new_ref, yt_ref):
    del prev_ref
    _state_kernel(dec_ref, st_ref, xst_ref, bm_ref, cmt_ref, new_ref, yt_ref)


def _sample_state(dec, state, layer, xst, bm, cmt, new_all):
    per_tile = SEQS_PER_TILE // STATE_SEQS
    in_specs = [pl.BlockSpec(memory_space=pltpu.SMEM),
                pl.BlockSpec((None, STATE_SEQS, D_SSD, STATE), lambda b: (layer, b, 0, 0)),
                pl.BlockSpec((D_SSD, TILE), lambda b: (0, b // per_tile)),
                pl.BlockSpec((TILE, GROUPS * STATE), lambda b: (b // per_tile, 0)),
                pl.BlockSpec((GROUPS * STATE, TILE), lambda b: (0, b // per_tile))]
    args = [dec, state, xst, bm, cmt]
    aliases = {}
    body = _state_kernel
    if new_all is not None:
        in_specs.append(pl.BlockSpec(memory_space=pl.ANY))
        args.append(new_all)
        aliases = {5: 0}
        body = _state_kernel_aliased
    return pl.pallas_call(
        body,
        grid=(DEC_BATCH // STATE_SEQS,),
        in_specs=in_specs,
        out_specs=[pl.BlockSpec((None, STATE_SEQS, D_SSD, STATE), lambda b: (layer, b, 0, 0)),
                   pl.BlockSpec((D_SSD, TILE), lambda b: (0, b // per_tile))],
        out_shape=[jax.ShapeDtypeStruct((DEPTH, DEC_BATCH, D_SSD, STATE), F32),
                   jax.ShapeDtypeStruct((D_SSD, T_SAMPLE), F32)],
        input_output_aliases=aliases,
        compiler_params=_cparams(1),
        name="sample_state",
    )(*args)


def _expansion_matrix():
    e = np.zeros((2 * LANES, D_SSD), np.float32)
    for h in range(HEADS):
        e[h, h * HEAD_DIM:(h + 1) * HEAD_DIM] = 1.0
        e[LANES + h, h * HEAD_DIM:(h + 1) * HEAD_DIM] = 1.0
    return jnp.asarray(e, BF16)


def kernel(x_prompt, x_sample, state_ssm, state_conv, p_prompt, p_sample, g_mix, w_in, conv_w, conv_b, dt_bias, a_log, d_skip, g_ssd, g_cm, w_s, b_s, w_out, g_ffn, w_gate, w_up, w_down, g_pg, w_pg, w_ple, g_final):
    xp, xs = x_prompt.reshape(T_PROMPT, D_MODEL), x_sample.reshape(T_SAMPLE, D_MODEL)
    h, h_tail = xp, jnp.concatenate([xp[T_PROMPT - TILE:], xs], axis=0)
    p_all = jnp.concatenate([p_prompt.reshape(DEPTH, T_PROMPT, D_PLE), p_sample.reshape(DEPTH, T_SAMPLE, D_PLE)], axis=1)
    state_all = state_ssm.reshape(DEPTH, DEC_BATCH, D_SSD, STATE)
    w_in_t = jnp.transpose(w_in, (0, 2, 1))
    pad_heads = ((0, 0), (0, 0), (0, LANES - HEADS))
    reps = TILE // DEC_SEQ
    params_common = (conv_w[:, :, :D_SSD], conv_w[:, :, D_SSD:], conv_b[:, None, :D_SSD], conv_b[:, None, D_SSD:],
                     jnp.pad(dt_bias[:, None, :], pad_heads), jnp.pad(a_log[:, None, :], pad_heads),
                     jnp.repeat(d_skip, HEAD_DIM, axis=1)[:, None, :], g_ssd[:, None, :], g_cm[:, None, :],
                     _expansion_matrix())
    params_p = params_common + (w_s, jnp.repeat(jnp.swapaxes(b_s, 1, 2), LANES, axis=2))
    params_s = params_common + (
        jnp.tile(w_s[:, :, :DEC_SEQ, :DEC_SEQ], (1, 1, reps, reps)),
        jnp.tile(jnp.repeat(jnp.swapaxes(b_s[:, :, :DEC_SEQ], 1, 2), LANES, axis=2), (1, reps, 1)))
    hists = jnp.transpose(state_conv, (0, 2, 1, 3))
    ssm_p, conv_p, conv_s, v_s = [], [], [], []
    ssm_s_all = None
    g_ffn3, g_pg3 = g_ffn[:, None, :], g_pg[:, None, :]
    hg, ss = _norm_stats(xp, xs, g_mix[0])
    for i in range(DEPTH):
        proj = _in_proj(hg, ss, w_in_t, i, IN_TM)
        dtraw = _in_proj_dt(hg, ss, w_in_t, i, IN_TM)

        w_out_bf = _cast_bf16(w_out, i)
        h_new, hg, ss, ssm_new_p, tail_p, ytail = _mix_prompt(proj, dtraw, params_p, w_out_bf, h, g_ffn3, i)
        xst, bmat, cmt, ea = _mix_sample_prep(proj, dtraw, hists, params_s, i)
        dec = ea[DEC_SEQ - 1::DEC_SEQ, :HEADS]
        ssm_s_all, yofft = _sample_state(dec, state_all, i, xst, bmat, cmt, ssm_s_all)
        ytail, vn_s = _mix_sample_final(proj, dtraw, hists, yofft, ytail, params_s, i)
        h, hg, ss = _out_proj_tail(ytail, w_out_bf, h_tail if i == 0 else h, g_ffn3, i, h_new, hg, ss)
        act = _ffn_up(hg, ss, w_gate, w_up, i, *FFN_UP_TILE)
        h, hg, ss = _mm_res(act, w_down, i, h, g_pg3, *FFN_DOWN_TILE, "ffn_down")
        if i + 1 < DEPTH:
            h, hg, ss = _ple(hg, ss, p_all, h, w_pg, w_ple, g_mix[i + 1].reshape(1, -1), i, *PLE_TILE)
        else:
            h, = _ple(hg, ss, p_all, h, w_pg, w_ple, None, i, *PLE_TILE)

        xbc_s = jnp.concatenate([lax.slice(proj, (T_PROMPT, D_SSD), (T_ALL, 2 * D_SSD)),
                                 lax.slice(proj, (T_PROMPT, 4 * 2048), (T_ALL, 4 * 2048 + BC_W))], axis=1)
        ssm_p.append(ssm_new_p.reshape(BATCH, HEADS, HEAD_DIM, STATE))
        conv_p.append(tail_p[:, -(CONV_W - 1):])
        conv_s.append(xbc_s.reshape(DEC_BATCH, DEC_SEQ, -1)[:, -(CONV_W - 1):])
        v_s.append(vn_s.reshape(DEC_BATCH, DEC_SEQ, D_CM))
    out_p = _rms(h, g_final, F32, tm=T_SAMPLE, rows=T_PROMPT)
    out_s = _rms(h, g_final, F32, tm=T_SAMPLE, row0=T_PROMPT, rows=T_SAMPLE)
    return (out_p.reshape(BATCH, SEQ, D_MODEL), out_s.reshape(DEC_BATCH, DEC_SEQ, D_MODEL),
            jnp.stack(ssm_p), jnp.stack(conv_p),
            ssm_s_all.reshape(DEPTH, DEC_BATCH, HEADS, HEAD_DIM, STATE), jnp.stack(conv_s), jnp.stack(v_s))
```
